```python
import math
import jax, jax.numpy as jnp
from jax import lax
import numpy as np

D_MODEL = 1024
BATCH = 16
SEQ = 2048
DEPTH = 2

CTX_LEN = 256
GRID_W = 64
N_MIXERS = 2
N_HEADS = 8
HEAD_DIM = 64
V_DIM = 2 * HEAD_DIM
QK_WIDTH = N_HEADS * 2 * HEAD_DIM
V_WIDTH = N_HEADS * V_DIM
ROPE_THETA = 10000.0
ROPE_PAIRS = HEAD_DIM // 4
N_FOURIER_GROUPS = 4
FOURIER_GROUP = D_MODEL // N_FOURIER_GROUPS
D_FF = 2816
N_MOD = 9
Q_BLOCK = 128
EPS = 1e-6
N_ATTN_LAYERS = (DEPTH + N_MIXERS - 1) // N_MIXERS
N_FOURIER_LAYERS = DEPTH // N_MIXERS

kernel_name = "hybrid_diffattn_fnet_macaron_dit_block"


def rms_norm(x, g):
    x32 = x.astype(jnp.float32)
    y = x32 * lax.rsqrt(jnp.mean(x32 * x32, axis=-1, keepdims=True) + EPS)
    return (y * g.astype(jnp.float32)).astype(x.dtype)


def modulate(h, shift, scale):
    return h * (1 + scale) + shift


def swiglu(h, w_gu, w_d):
    g, u = jnp.split(h @ w_gu, 2, axis=-1)
    return (jax.nn.silu(g) * u) @ w_d


def ada_params(cond, w_mod, b_mod):
    m = jax.nn.silu(cond) @ w_mod + b_mod
    m = m.reshape(m.shape[:-1] + (N_MOD, 1, D_MODEL))
    return [m[..., k, :, :] for k in range(N_MOD)]


def axial_rope_tables(n_tokens):
    rows = n_tokens // GRID_W
    row = jnp.repeat(jnp.arange(rows, dtype=jnp.float32), GRID_W)
    col = jnp.tile(jnp.arange(GRID_W, dtype=jnp.float32), rows)
    inv_freq = ROPE_THETA ** (-(jnp.arange(ROPE_PAIRS, dtype=jnp.float32) / ROPE_PAIRS))
    ang = jnp.concatenate([row[:, None] * inv_freq, col[:, None] * inv_freq], axis=-1)
    ang = ang.reshape(n_tokens, 2, ROPE_PAIRS)
    return jnp.cos(ang), jnp.sin(ang)


def apply_axial_rope(t, cos, sin):
    shp = t.shape
    t = t.reshape(shp[:-1] + (2, 2, ROPE_PAIRS))
    t1, t2 = t[..., 0, :], t[..., 1, :]
    cs = cos.astype(t.dtype)[:, None, None]
    sn = sin.astype(t.dtype)[:, None, None]
    out = jnp.stack([t1 * cs - t2 * sn, t2 * cs + t1 * sn], axis=-2)
    return out.reshape(shp)


def diff_attn_core(q, k, v, lam):
    s = jnp.einsum('bqhcd,bkhcd->bhcqk', q.astype(jnp.float32), k.astype(jnp.float32))
    p = jax.nn.softmax(s * (1.0 / math.sqrt(HEAD_DIM)), axis=-1)
    a = p[:, :, 0] - lam * p[:, :, 1]
    return jnp.einsum('bhqk,bkhe->bqhe', a.astype(v.dtype), v)


def split_qkv(h, w_qkv):
    b, n, _ = h.shape
    qkv = h @ w_qkv
    q = qkv[..., :QK_WIDTH].reshape(b, n, N_HEADS, 2, HEAD_DIM)
    k = qkv[..., QK_WIDTH:2 * QK_WIDTH].reshape(b, n, N_HEADS, 2, HEAD_DIM)
    v = qkv[..., 2 * QK_WIDTH:].reshape(b, n, N_HEADS, V_DIM)
    return q, k, v


def diff_head_out(o, sub_g, w_o, lambda_init):
    b, n = o.shape[:2]
    o = rms_norm(o, sub_g) * (1.0 - lambda_init)
    return o.reshape(b, n, V_WIDTH) @ w_o


def diff_attention(hx, hy, w_qkv, w_o, q_g, k_g, lq1, lk1, lq2, lk2, sub_g,
                   lambda_init, cos, sin, with_ctx_queries):
    b, n, _ = hx.shape
    lam = (jnp.exp(jnp.sum(lq1.astype(jnp.float32) * lk1.astype(jnp.float32)))
           - jnp.exp(jnp.sum(lq2.astype(jnp.float32) * lk2.astype(jnp.float32)))
           + lambda_init)
    q, k, v = split_qkv(hx, w_qkv)
    q = apply_axial_rope(rms_norm(q, q_g), cos, sin)
    k = apply_axial_rope(rms_norm(k, k_g), cos, sin)
    qy, ky, vy = split_qkv(hy, w_qkv)
    ky = rms_norm(ky, k_g)
    keys = jnp.concatenate([ky, k], axis=1)
    vals = jnp.concatenate([vy, v], axis=1)
    nb = n // Q_BLOCK
    qb = q.reshape(b, nb, Q_BLOCK, N_HEADS, 2, HEAD_DIM).swapaxes(0, 1)
    ob = lax.map(lambda qq: diff_attn_core(qq, keys, vals, lam), qb)
    o = ob.swapaxes(0, 1).reshape(b, n, N_HEADS, V_DIM)
    out_x = diff_head_out(o, sub_g, w_o, lambda_init)
    out_y = None
    if with_ctx_queries:
        qy = rms_norm(qy, q_g)
        oy = diff_attn_core(qy, ky, vy, lam)
        out_y = diff_head_out(oy, sub_g, w_o, lambda_init)
    return out_x, out_y


def fourier_mix(h, w_f):
    b, n, d = h.shape
    hg = h.astype(jnp.float32).reshape(b, n, N_FOURIER_GROUPS, FOURIER_GROUP)
    f = jnp.fft.fft2(hg, axes=(1, 3), norm="ortho").real
    return f.reshape(b, n, d).astype(h.dtype) @ w_f


def setup_inputs(seed: int = 0) -> dict:
    key = jax.random.key(seed)
    ks = jax.random.split(key, 24)
    nrm = jax.random.normal
    f32 = jnp.float32
    NA, NF = N_ATTN_LAYERS, N_FOURIER_LAYERS
    return {
        "x": nrm(ks[0], (BATCH, SEQ, D_MODEL), f32),
        "c": nrm(ks[1], (BATCH, D_MODEL), f32),
        "ctx": nrm(ks[2], (BATCH, CTX_LEN, D_MODEL), f32),
        "c_ctx": nrm(ks[3], (D_MODEL,), f32),
        "norm_g": 1.0 + 0.05 * nrm(ks[4], (DEPTH, 3, D_MODEL), f32),
        "w_mod": 0.5 * nrm(ks[5], (DEPTH, D_MODEL, N_MOD * D_MODEL), f32) * D_MODEL ** -0.5,
        "b_mod": 0.01 * nrm(ks[6], (DEPTH, N_MOD * D_MODEL), f32),
        "ffn1_w_gu": nrm(ks[7], (DEPTH, D_MODEL, 2 * D_FF), f32) * D_MODEL ** -0.5,
        "ffn1_w_d": nrm(ks[8], (DEPTH, D_FF, D_MODEL), f32) * D_FF ** -0.5,
        "ffn2_w_gu": nrm(ks[9], (DEPTH, D_MODEL, 2 * D_FF), f32) * D_MODEL ** -0.5,
        "ffn2_w_d": nrm(ks[10], (DEPTH, D_FF, D_MODEL), f32) * D_FF ** -0.5,
        "attn_w_qkv": nrm(ks[11], (NA, D_MODEL, 2 * QK_WIDTH + V_WIDTH), f32) * D_MODEL ** -0.5,
        "attn_w_o": nrm(ks[12], (NA, V_WIDTH, D_MODEL), f32) * V_WIDTH ** -0.5,
        "attn_q_g": 1.0 + 0.05 * nrm(ks[13], (NA, HEAD_DIM), f32),
        "attn_k_g": 1.0 + 0.05 * nrm(ks[14], (NA, HEAD_DIM), f32),
        "attn_lam_q1": 0.1 * nrm(ks[15], (NA, HEAD_DIM), f32),
        "attn_lam_k1": 0.1 * nrm(ks[16], (NA, HEAD_DIM), f32),
        "attn_lam_q2": 0.1 * nrm(ks[17], (NA, HEAD_DIM), f32),
        "attn_lam_k2": 0.1 * nrm(ks[18], (NA, HEAD_DIM), f32),
        "attn_sub_g": 1.0 + 0.05 * nrm(ks[19], (NA, V_DIM), f32),
        "fourier_w": nrm(ks[20], (NF, D_MODEL, D_MODEL), f32) * D_MODEL ** -0.5,
    }


def reference(x, c, ctx, c_ctx, norm_g, w_mod, b_mod, ffn1_w_gu, ffn1_w_d, ffn2_w_gu,
              ffn2_w_d, attn_w_qkv, attn_w_o, attn_q_g, attn_k_g, attn_lam_q1, attn_lam_k1,
              attn_lam_q2, attn_lam_k2, attn_sub_g, fourier_w):
    n_lat = x.shape[1]
    cos, sin = axial_rope_tables(n_lat)
    y = ctx
    for i in range(DEPTH):
        kind = i % N_MIXERS
        ctx_needed_after = any(j % N_MIXERS == 0 for j in range(i + 1, DEPTH))
        ctx_in_layer = ctx_needed_after or kind == 0
        mx = ada_params(c, w_mod[i], b_mod[i])
        x = x + 0.5 * mx[2] * swiglu(modulate(rms_norm(x, norm_g[i, 0]), mx[0], mx[1]),
                                     ffn1_w_gu[i], ffn1_w_d[i])
        if ctx_in_layer:
            my = ada_params(c_ctx, w_mod[i], b_mod[i])
            y = y + 0.5 * my[2] * swiglu(modulate(rms_norm(y, norm_g[i, 0]), my[0], my[1]),
                                         ffn1_w_gu[i], ffn1_w_d[i])
        hx = modulate(rms_norm(x, norm_g[i, 1]), mx[3], mx[4])
        if kind == 0:
            a = i // N_MIXERS
            hy = modulate(rms_norm(y, norm_g[i, 1]), my[3], my[4])
            lambda_init = 0.8 - 0.6 * math.exp(-0.3 * i)
            ox, oy = diff_attention(hx, hy, attn_w_qkv[a], attn_w_o[a], attn_q_g[a],
                                    attn_k_g[a], attn_lam_q1[a], attn_lam_k1[a],
                                    attn_lam_q2[a], attn_lam_k2[a], attn_sub_g[a],
                                    lambda_init, cos, sin, ctx_needed_after)
            x = x + mx[5] * ox
            if ctx_needed_after:
                y = y + my[5] * oy
        else:
            f = i // N_MIXERS
            x = x + mx[5] * fourier_mix(hx, fourier_w[f])
            if ctx_needed_after:
                hy = modulate(rms_norm(y, norm_g[i, 1]), my[3], my[4])
                y = y + my[5] * fourier_mix(hy, fourier_w[f])
        x = x + 0.5 * mx[8] * swiglu(modulate(rms_norm(x, norm_g[i, 2]), mx[6], mx[7]),
                                     ffn2_w_gu[i], ffn2_w_d[i])
        if ctx_needed_after:
            y = y + 0.5 * my[8] * swiglu(modulate(rms_norm(y, norm_g[i, 2]), my[6], my[7]),
                                         ffn2_w_gu[i], ffn2_w_d[i])
    return x
```

```python
import functools
import math

import numpy as np
import jax
import jax.numpy as jnp
from jax import lax
from jax.experimental import pallas as pl
from jax.experimental.pallas import tpu as pltpu

F32 = jnp.float32
BF16 = jnp.bfloat16

D_MODEL = 1024
N_MOD = 9
N_HEADS = 8
HEAD_DIM = 64
V_DIM = 2 * HEAD_DIM
GRID_W = 64
ROPE_THETA = 10000.0
ROPE_PAIRS = HEAD_DIM // 4
N_FOURIER_GROUPS = 4
FOURIER_GROUP = D_MODEL // N_FOURIER_GROUPS
D_FF = 2816
EPS = 1e-6

LANES = 128
MXU_DIM = 256
COND_ROWS = 24
VMEM_LIMIT = 56 * 1024 * 1024

FFN_TOKENS = 512
FFN_CHUNK = 256
QKV_TOKENS = 512
ATTN_Q_TILE = 512
ADA_COLS = 1024


def _resident(shape):
    nd = len(shape)
    return pl.BlockSpec(shape, lambda *_: (0,) * nd, pipeline_mode=pl.Buffered(1))


def _params(n_axes):
    return pltpu.CompilerParams(dimension_semantics=("arbitrary",) * n_axes,
                                vmem_limit_bytes=VMEM_LIMIT)


def _ada_kernel(cond_ref, w_ref, b_ref, o_ref):
    cnd = cond_ref[...]
    act = (cnd * jax.nn.sigmoid(cnd)).astype(BF16)
    o_ref[0] = jnp.dot(act, w_ref[0].astype(BF16), preferred_element_type=F32) + b_ref[0]


def _ada_call(cond, w_mod, b_mod):
    depth, _, width = w_mod.shape
    return pl.pallas_call(
        _ada_kernel,
        grid=(depth, width // ADA_COLS),
        in_specs=[
            pl.BlockSpec((COND_ROWS, D_MODEL), lambda i, j: (0, 0)),
            pl.BlockSpec((1, D_MODEL, ADA_COLS), lambda i, j: (i, 0, j)),
            pl.BlockSpec((1, 1, ADA_COLS), lambda i, j: (i, 0, j)),
        ],
        out_specs=pl.BlockSpec((1, COND_ROWS, ADA_COLS), lambda i, j: (i, 0, j)),
        out_shape=jax.ShapeDtypeStruct((depth, COND_ROWS, width), F32),
        compiler_params=_params(2),
        name="ada",
    )(cond, w_mod, b_mod.reshape(depth, 1, width))


def _rms_mod(x, gain, shift, scale):
    ms = jnp.mean(x * x, axis=-1, keepdims=True)
    return x * lax.rsqrt(ms + EPS) * (gain * (1.0 + scale)) + shift


def _ffn_kernel(*refs, premix, emit_h, ffn_idx):
    x_ref, mod_ref, gain_ref = refs[:3]
    refs = refs[3:]
    if premix:
        t_ref, wt_ref = refs[:2]
        refs = refs[2:]
    wgu_ref, wd_ref, out_ref = refs[:3]
    x = x_ref[0]
    if premix:
        x = x + mod_ref[0, 5:6, :] * jnp.dot(t_ref[0], wt_ref[...], preferred_element_type=F32)
    ks = 6 if ffn_idx else 0
    grow = 2 if ffn_idx else 0
    h = _rms_mod(x, gain_ref[grow:grow + 1, :], mod_ref[0, ks:ks + 1, :],
                 mod_ref[0, ks + 1:ks + 2, :]).astype(BF16)
    acc = jnp.zeros(x.shape, F32)
    for c in range(D_FF // FFN_CHUNK):
        lo = c * FFN_CHUNK
        g = jnp.dot(h, wgu_ref[:, lo:lo + FFN_CHUNK], preferred_element_type=F32)
        u = jnp.dot(h, wgu_ref[:, D_FF + lo:D_FF + lo + FFN_CHUNK], preferred_element_type=F32)
        a = (g * jax.nn.sigmoid(g) * u).astype(BF16)
        acc = acc + jnp.dot(a, wd_ref[lo:lo + FFN_CHUNK, :], preferred_element_type=F32)
    out = x + (0.5 * mod_ref[0, ks + 2:ks + 3, :]) * acc
    out_ref[0] = out
    if emit_h:
        refs[3][0] = _rms_mod(out, gain_ref[1:2, :], mod_ref[0, 3:4, :],
                              mod_ref[0, 4:5, :]).astype(BF16)


def _ffn_call(x, mod, gains, w_gu, w_d, *, ffn_idx, mix=None, emit_h=False):
    b, t, d = x.shape
    tm = min(FFN_TOKENS, t)
    tok = pl.BlockSpec((1, tm, d), lambda i, j: (i, j, 0))
    mod_map = (lambda i, j: (i, 0, 0)) if mod.shape[0] == b else (lambda i, j: (0, 0, 0))
    in_specs = [tok, pl.BlockSpec((1, N_MOD, d), mod_map), _resident(gains.shape)]
    args = [x, mod, gains]
    if mix is not None:
        in_specs += [tok, _resident(mix[1].shape)]
        args += [mix[0], mix[1]]
    in_specs += [_resident(w_gu.shape), _resident(w_d.shape)]
    args += [w_gu, w_d]
    out_shape = [jax.ShapeDtypeStruct(x.shape, F32)]
    out_specs = [tok]
    if emit_h:
        out_shape.append(jax.ShapeDtypeStruct(x.shape, BF16))
        out_specs.append(tok)
    res = pl.pallas_call(
        functools.partial(_ffn_kernel, premix=mix is not None, emit_h=emit_h, ffn_idx=ffn_idx),
        grid=(b, t // tm),
        in_specs=in_specs,
        out_specs=out_specs,
        out_shape=out_shape,
        compiler_params=_params(2),
        name="ffn",
    )(*args)
    return res if emit_h else res[0]


def _group_ones():
    r = lax.broadcasted_iota(jnp.int32, (MXU_DIM, MXU_DIM), 0) // HEAD_DIM
    c = lax.broadcasted_iota(jnp.int32, (MXU_DIM, MXU_DIM), 1) // HEAD_DIM
    return (r == c).astype(BF16)


def _qk_norm_rope(t, gain, cos_g, sin_g, first_half, ones_bd, out_ref):
    sq = (t * t).astype(BF16)
    for j in range(t.shape[1] // MXU_DIM):
        ssq = jnp.dot(sq[:, j * MXU_DIM:(j + 1) * MXU_DIM], ones_bd, preferred_element_type=F32)
        inv = lax.rsqrt(ssq * (1.0 / HEAD_DIM) + EPS)
        for jj in range(MXU_DIM // LANES):
            lo = j * MXU_DIM + jj * LANES
            tb = t[:, lo:lo + LANES]
            ib = inv[:, jj * LANES:(jj + 1) * LANES]
            if cos_g is None:
                ob = tb * ib * gain
            else:
                partner = jnp.where(first_half, pltpu.roll(tb, LANES - 16, 1), pltpu.roll(tb, 16, 1))
                ob = ib * (tb * cos_g + partner * sin_g)
            out_ref[0, :, lo:lo + LANES] = ob.astype(BF16)


def _qkv_kernel(h_ref, w_ref, gq_ref, gk_ref, cos_ref, sin_ref, *out_refs, rope, q_scale):
    h = h_ref[0]
    ones_bd = _group_ones()
    first_half = None
    if rope:
        lane = lax.broadcasted_iota(jnp.int32, (h.shape[0], LANES), 1)
        first_half = (lane % 32) < 16
        cos_t = cos_ref[...]
        sin_t = sin_ref[...]

        def partner_gain(g_ref):
            g8 = jnp.broadcast_to(g_ref[...], (8, LANES))
            fh = (lax.broadcasted_iota(jnp.int32, (8, LANES), 1) % 32) < 16
            return jnp.where(fh, pltpu.roll(g8, LANES - 16, 1), pltpu.roll(g8, 16, 1))[:1]

        q_ref, k_ref, v_ref = out_refs
        q = jnp.dot(h, w_ref[:, :D_MODEL], preferred_element_type=F32)
        _qk_norm_rope(q, None, cos_t * (gq_ref[...] * q_scale), sin_t * (partner_gain(gq_ref) * q_scale),
                      first_half, ones_bd, q_ref)
        k = jnp.dot(h, w_ref[:, D_MODEL:2 * D_MODEL], preferred_element_type=F32)
        _qk_norm_rope(k, None, cos_t * gk_ref[...], sin_t * partner_gain(gk_ref),
                      first_half, ones_bd, k_ref)
    else:
        k_ref, v_ref = out_refs
        k = jnp.dot(h, w_ref[:, D_MODEL:2 * D_MODEL], preferred_element_type=F32)
        _qk_norm_rope(k, gk_ref[...], None, None, None, ones_bd, k_ref)
    v_ref[0] = jnp.dot(h, w_ref[:, 2 * D_MODEL:], preferred_element_type=F32).astype(BF16)


def _qkv_call(h, w_qkv, gq, gk, cos_t, sin_t, *, rope, q_scale):
    b, t, d = h.shape
    tm = min(QKV_TOKENS, t)
    tok = pl.BlockSpec((1, tm, d), lambda i, j: (i, j, 0))
    tab = pl.BlockSpec((tm, LANES), lambda i, j: (j, 0))
    n_out = 3 if rope else 2
    return pl.pallas_call(
        functools.partial(_qkv_kernel, rope=rope, q_scale=q_scale),
        grid=(b, t // tm),
        in_specs=[tok, _resident(w_qkv.shape), _resident(gq.shape), _resident(gk.shape), tab, tab],
        out_specs=[tok] * n_out,
        out_shape=[jax.ShapeDtypeStruct((b, t, d), BF16)] * n_out,
        compiler_params=_params(2),
        name="qkv",
    )(h, w_qkv, gq, gk, cos_t, sin_t)


def _attn_kernel(q_ref, kl_ref, vl_ref, kc_ref, vc_ref, lq1_ref, lk1_ref, lq2_ref, lk2_ref,
                 subg_ref, o_ref, kcat, vext, *, lambda_init, n_ctx):
    @pl.when(pl.program_id(2) == 0)
    def _():
        kcat[0:n_ctx, :] = kc_ref[0]
        kcat[n_ctx:, :] = kl_ref[0]
        vext[0:n_ctx, 0:V_DIM] = vc_ref[0]
        vext[n_ctx:, 0:V_DIM] = vl_ref[0]
        vext[:, V_DIM:] = jnp.ones((vext.shape[0], V_DIM), BF16)

    lam = (jnp.exp(jnp.sum(lq1_ref[...] * lk1_ref[...], axis=-1, keepdims=True))
           - jnp.exp(jnp.sum(lq2_ref[...] * lk2_ref[...], axis=-1, keepdims=True))
           + lambda_init)
    q = q_ref[0]
    lane = lax.broadcasted_iota(jnp.int32, q.shape, 1)
    k = kcat[...]
    v = vext[...]
    outs = []
    for comp in range(2):
        sel = (lane < HEAD_DIM) if comp == 0 else (lane >= HEAD_DIM)
        qm = jnp.where(sel, q, jnp.zeros_like(q))
        s = lax.dot_general(qm, k, (((1,), (1,)), ((), ())), preferred_element_type=F32)
        m = jnp.max(s, axis=-1, keepdims=True)
        e = jnp.exp2(s - m).astype(BF16)
        r = jnp.dot(e, v, preferred_element_type=F32)
        outs.append(r[:, :V_DIM] / r[:, V_DIM:])
    o = outs[0] - lam * outs[1]
    ms = jnp.mean(o * o, axis=-1, keepdims=True)
    o_ref[0] = (o * lax.rsqrt(ms + EPS) * (subg_ref[...] * (1.0 - lambda_init))).astype(BF16)


def _attn_call(q, k_lat, v_lat, k_ctx, v_ctx, lq1, lk1, lq2, lk2, sub_g, *, lambda_init):
    b, s, _ = q.shape
    n_ctx = k_ctx.shape[1]
    tq = ATTN_Q_TILE
    qspec = pl.BlockSpec((1, tq, V_DIM), lambda i, h, j: (i, j, h))
    lat = pl.BlockSpec((1, s, V_DIM), lambda i, h, j: (i, 0, h))
    ctx = pl.BlockSpec((1, n_ctx, V_DIM), lambda i, h, j: (i, 0, h))
    vec = pl.BlockSpec((1, HEAD_DIM), lambda i, h, j: (0, 0))
    return pl.pallas_call(
        functools.partial(_attn_kernel, lambda_init=lambda_init, n_ctx=n_ctx),
        grid=(b, N_HEADS, s // tq),
        in_specs=[qspec, lat, lat, ctx, ctx, vec, vec, vec, vec,
                  pl.BlockSpec((1, V_DIM), lambda i, h, j: (0, 0))],
        out_specs=qspec,
        out_shape=jax.ShapeDtypeStruct(q.shape, BF16),
        scratch_shapes=[pltpu.VMEM((n_ctx + s, V_DIM), BF16),
                        pltpu.VMEM((n_ctx + s, 2 * V_DIM), BF16)],
        compiler_params=_params(3),
        name="attn",
    )(q, k_lat, v_lat, k_ctx, v_ctx, lq1, lk1, lq2, lk2, sub_g)


def _fourier_kernel(he_ref, ho_ref, w1_ref, tab_ref, o_ref, w1_bf, tab_bf):
    @pl.when((pl.program_id(0) == 0) & (pl.program_id(1) == 0))
    def _():
        w1_bf[...] = w1_ref[...].astype(BF16)
        for i in range(tab_ref.shape[0]):
            tab_bf[i] = tab_ref[i].astype(BF16)

    w1 = w1_bf[...]
    ze = jnp.dot(he_ref[0], w1, preferred_element_type=F32).astype(BF16)
    zo = jnp.dot(ho_ref[0], w1, preferred_element_type=F32).astype(BF16)
    m = FOURIER_GROUP
    e = (jnp.dot(tab_bf[0], ze[:, :m], preferred_element_type=F32)
         + jnp.dot(tab_bf[1], ze[:, m:], preferred_element_type=F32))
    o = (jnp.dot(tab_bf[2], zo[:, :m], preferred_element_type=F32)
         + jnp.dot(tab_bf[3], zo[:, m:], preferred_element_type=F32))
    n2 = e.shape[0]
    o_ref[0, :n2, :] = (e + o).astype(BF16)
    o_ref[0, n2:, :] = (e - o).astype(BF16)


def _fourier_call(h, w1, tab):
    b, s, d = h.shape
    groups = d // FOURIER_GROUP
    h2 = h.reshape(b, s // 2, 2 * d)
    return pl.pallas_call(
        _fourier_kernel,
        grid=(b, groups),
        in_specs=[pl.BlockSpec((1, s // 2, FOURIER_GROUP), lambda i, g: (i, 0, g)),
                  pl.BlockSpec((1, s // 2, FOURIER_GROUP), lambda i, g: (i, 0, groups + g)),
                  _resident(w1.shape), _resident(tab.shape)],
        out_specs=pl.BlockSpec((1, s, FOURIER_GROUP), lambda i, g: (i, 0, g)),
        out_shape=jax.ShapeDtypeStruct(h.shape, BF16),
        scratch_shapes=[pltpu.VMEM(w1.shape, BF16), pltpu.VMEM(tab.shape, BF16)],
        compiler_params=_params(2),
        name="fourier",
    )(h2, h2, w1, tab)


def _dft_tables(n_pos):
    m = FOURIER_GROUP
    km = (np.outer(np.arange(m), np.arange(m)) % m) * (2.0 * np.pi / m)
    w1 = np.concatenate([np.cos(km), np.sin(km)], axis=1) / math.sqrt(m)
    k = np.arange(n_pos // 2)
    tabs = []
    for parity in range(2):
        kn = (np.outer(k, 2 * k + parity) % n_pos) * (2.0 * np.pi / n_pos)
        tabs += [np.cos(kn) / math.sqrt(n_pos), -np.sin(kn) / math.sqrt(n_pos)]
    return jnp.asarray(w1, F32), jnp.asarray(np.stack(tabs), F32)


def _rope_tables(n_pos):
    pos = np.arange(n_pos)
    inv_freq = ROPE_THETA ** (-(np.arange(ROPE_PAIRS, dtype=np.float64) / ROPE_PAIRS))
    lane = np.arange(LANES)
    axis = (lane % HEAD_DIM) // 32
    coord = np.where(axis[None, :] == 0, (pos // GRID_W)[:, None], (pos % GRID_W)[:, None])
    ang = coord * inv_freq[lane % ROPE_PAIRS][None, :]
    sign = np.where((lane % 32) < 16, -1.0, 1.0)[None, :]
    return jnp.asarray(np.cos(ang), F32), jnp.asarray(np.sin(ang) * sign, F32)


def kernel(x, c, ctx, c_ctx, norm_g, w_mod, b_mod, ffn1_w_gu, ffn1_w_d, ffn2_w_gu, ffn2_w_d,
           attn_w_qkv, attn_w_o, attn_q_g, attn_k_g, attn_lam_q1, attn_lam_k1, attn_lam_q2,
           attn_lam_k2, attn_sub_g, fourier_w):
    b, s, d = x.shape
    cond = jnp.concatenate([c, c_ctx[None, :], jnp.zeros((COND_ROWS - b - 1, d), F32)], axis=0)
    mods = _ada_call(cond, w_mod, b_mod)
    mx = [mods[i, :b].reshape(b, N_MOD, d) for i in range(2)]
    my0 = mods[0, b:b + 1].reshape(1, N_MOD, d)

    wgu1, wd1 = ffn1_w_gu.astype(BF16), ffn1_w_d.astype(BF16)
    wgu2, wd2 = ffn2_w_gu.astype(BF16), ffn2_w_d.astype(BF16)
    w_qkv = attn_w_qkv[0].astype(BF16)
    w_o = attn_w_o[0].astype(BF16)
    w_f = fourier_w[0].astype(BF16)

    x, hx = _ffn_call(x, mx[0], norm_g[0], wgu1[0], wd1[0], ffn_idx=0, emit_h=True)
    _, hy = _ffn_call(ctx, my0, norm_g[0], wgu1[0], wd1[0], ffn_idx=0, emit_h=True)
    cos_t, sin_t = _rope_tables(s)
    gq = jnp.tile(attn_q_g[0], 2)[None, :]
    gk = jnp.tile(attn_k_g[0], 2)[None, :]
    q_scale = math.log2(math.e) / math.sqrt(HEAD_DIM)
    q, k, v = _qkv_call(hx, w_qkv, gq, gk, cos_t, sin_t, rope=True, q_scale=q_scale)
    ky, vy = _qkv_call(hy, w_qkv, gq, gk, cos_t, sin_t, rope=False, q_scale=1.0)
    lambda_init = 0.8 - 0.6 * math.exp(-0.3 * 0)
    o = _attn_call(q, k, v, ky, vy, attn_lam_q1[0:1], attn_lam_k1[0:1], attn_lam_q2[0:1],
                   attn_lam_k2[0:1], attn_sub_g[0:1], lambda_init=lambda_init)
    x = _ffn_call(x, mx[0], norm_g[0], wgu2[0], wd2[0], ffn_idx=1, mix=(o, w_o))

    x, hx = _ffn_call(x, mx[1], norm_g[1], wgu1[1], wd1[1], ffn_idx=0, emit_h=True)
    f = _fourier_call(hx, *_dft_tables(s))
    x = _ffn_call(x, mx[1], norm_g[1], wgu2[1], wd2[1], ffn_idx=1, mix=(f, w_f))
    return x
```

```python
import functools
import math

import numpy as np
import jax
import jax.numpy as jnp
from jax import lax
from jax.experimental import pallas as pl
from jax.experimental.pallas import tpu as pltpu

F32 = jnp.float32
BF16 = jnp.bfloat16

D_MODEL = 1024
N_MOD = 9
N_HEADS = 8
HEAD_DIM = 64
V_DIM = 2 * HEAD_DIM
GRID_W = 64
ROPE_THETA = 10000.0
ROPE_PAIRS = HEAD_DIM // 4
N_FOURIER_GROUPS = 4
FOURIER_GROUP = D_MODEL // N_FOURIER_GROUPS
D_FF = 2816
EPS = 1e-6
EXP2_SAFE = 100.0

LANES = 128
MXU_DIM = 256
COND_ROWS = 24
VMEM_LIMIT = 56 * 1024 * 1024

FFN_TOKENS = 512
FFN_CHUNK = 256
QKV_TOKENS = 512
ATTN_Q_TILE = 512
ADA_COLS = 1024


def _resident(shape):
    nd = len(shape)
    return pl.BlockSpec(shape, lambda *_: (0,) * nd, pipeline_mode=pl.Buffered(1))


def _params(n_axes):
    return pltpu.CompilerParams(dimension_semantics=("arbitrary",) * n_axes,
                                vmem_limit_bytes=VMEM_LIMIT)


def _ada_kernel(cond_ref, w_ref, b_ref, o_ref):
    cnd = cond_ref[...]
    act = (cnd * jax.nn.sigmoid(cnd)).astype(BF16)
    o_ref[0] = jnp.dot(act, w_ref[0].astype(BF16), preferred_element_type=F32) + b_ref[0]


def _ada_call(cond, w_mod, b_mod):
    depth, _, width = w_mod.shape
    return pl.pallas_call(
        _ada_kernel,
        grid=(depth, width // ADA_COLS),
        in_specs=[
            pl.BlockSpec((COND_ROWS, D_MODEL), lambda i, j: (0, 0)),
            pl.BlockSpec((1, D_MODEL, ADA_COLS), lambda i, j: (i, 0, j)),
            pl.BlockSpec((1, 1, ADA_COLS), lambda i, j: (i, 0, j)),
        ],
        out_specs=pl.BlockSpec((1, COND_ROWS, ADA_COLS), lambda i, j: (i, 0, j)),
        out_shape=jax.ShapeDtypeStruct((depth, COND_ROWS, width), F32),
        compiler_params=_params(2),
        name="ada",
    )(cond, w_mod, b_mod.reshape(depth, 1, width))


def _rms_mod(x, gain, shift, scale):
    ms = jnp.mean(x * x, axis=-1, keepdims=True)
    return x * lax.rsqrt(ms + EPS) * (gain * (1.0 + scale)) + shift


def _ffn_kernel(*refs, premix, emit_h, ffn_idx):
    x_ref, mod_ref, gain_ref = refs[:3]
    refs = refs[3:]
    if premix:
        t_ref, wt_ref = refs[:2]
        refs = refs[2:]
    wgu_ref, wd_ref, out_ref = refs[:3]
    x = x_ref[0]
    if premix:
        x = x + mod_ref[0, 5:6, :] * jnp.dot(t_ref[0], wt_ref[...], preferred_element_type=F32)
    ks = 6 if ffn_idx else 0
    grow = 2 if ffn_idx else 0
    h = _rms_mod(x, gain_ref[grow:grow + 1, :], mod_ref[0, ks:ks + 1, :],
                 mod_ref[0, ks + 1:ks + 2, :]).astype(BF16)
    acc = jnp.zeros(x.shape, F32)
    for c in range(D_FF // FFN_CHUNK):
        lo = c * FFN_CHUNK
        g = jnp.dot(h, wgu_ref[:, lo:lo + FFN_CHUNK], preferred_element_type=F32)
        u = jnp.dot(h, wgu_ref[:, D_FF + lo:D_FF + lo + FFN_CHUNK], preferred_element_type=F32)
        a = (g * jax.nn.sigmoid(g) * u).astype(BF16)
        acc = acc + jnp.dot(a, wd_ref[lo:lo + FFN_CHUNK, :], preferred_element_type=F32)
    out = x + (0.5 * mod_ref[0, ks + 2:ks + 3, :]) * acc
    out_ref[0] = out
    if emit_h:
        refs[3][0] = _rms_mod(out, gain_ref[1:2, :], mod_ref[0, 3:4, :],
                              mod_ref[0, 4:5, :]).astype(BF16)


def _ffn_call(x, mod, gains, w_gu, w_d, *, ffn_idx, mix=None, emit_h=False):
    b, t, d = x.shape
    tm = min(FFN_TOKENS, t)
    tok = pl.BlockSpec((1, tm, d), lambda i, j: (i, j, 0))
    mod_map = (lambda i, j: (i, 0, 0)) if mod.shape[0] == b else (lambda i, j: (0, 0, 0))
    in_specs = [tok, pl.BlockSpec((1, N_MOD, d), mod_map), _resident(gains.shape)]
    args = [x, mod, gains]
    if mix is not None:
        in_specs += [tok, _resident(mix[1].shape)]
        args += [mix[0], mix[1]]
    in_specs += [_resident(w_gu.shape), _resident(w_d.shape)]
    args += [w_gu, w_d]
    out_shape = [jax.ShapeDtypeStruct(x.shape, F32)]
    out_specs = [tok]
    if emit_h:
        out_shape.append(jax.ShapeDtypeStruct(x.shape, BF16))
        out_specs.append(tok)
    res = pl.pallas_call(
        functools.partial(_ffn_kernel, premix=mix is not None, emit_h=emit_h, ffn_idx=ffn_idx),
        grid=(b, t // tm),
        in_specs=in_specs,
        out_specs=out_specs,
        out_shape=out_shape,
        compiler_params=_params(2),
        name="ffn",
    )(*args)
    return res if emit_h else res[0]


def _group_ones():
    r = lax.broadcasted_iota(jnp.int32, (MXU_DIM, MXU_DIM), 0) // HEAD_DIM
    c = lax.broadcasted_iota(jnp.int32, (MXU_DIM, MXU_DIM), 1) // HEAD_DIM
    return (r == c).astype(BF16)


def _qk_norm_rope(t, gain, cos_g, sin_g, first_half, ones_bd, out_ref):
    sq = (t * t).astype(BF16)
    for j in range(t.shape[1] // MXU_DIM):
        ssq = jnp.dot(sq[:, j * MXU_DIM:(j + 1) * MXU_DIM], ones_bd, preferred_element_type=F32)
        inv = lax.rsqrt(ssq * (1.0 / HEAD_DIM) + EPS)
        for jj in range(MXU_DIM // LANES):
            lo = j * MXU_DIM + jj * LANES
            tb = t[:, lo:lo + LANES]
            ib = inv[:, jj * LANES:(jj + 1) * LANES]
            if cos_g is None:
                ob = tb * ib * gain
            else:
                partner = jnp.where(first_half, pltpu.roll(tb, LANES - 16, 1), pltpu.roll(tb, 16, 1))
                ob = ib * (tb * cos_g + partner * sin_g)
            out_ref[0, :, lo:lo + LANES] = ob.astype(BF16)


def _qkv_kernel(h_ref, w_ref, gq_ref, gk_ref, cos_ref, sin_ref, *out_refs, rope, q_scale):
    h = h_ref[0]
    ones_bd = _group_ones()
    first_half = None
    if rope:
        lane = lax.broadcasted_iota(jnp.int32, (h.shape[0], LANES), 1)
        first_half = (lane % 32) < 16
        cos_t = cos_ref[...]
        sin_t = sin_ref[...]

        def partner_gain(g_ref):
            g8 = jnp.broadcast_to(g_ref[...], (8, LANES))
            fh = (lax.broadcasted_iota(jnp.int32, (8, LANES), 1) % 32) < 16
            return jnp.where(fh, pltpu.roll(g8, LANES - 16, 1), pltpu.roll(g8, 16, 1))[:1]

        q_ref, k_ref, v_ref = out_refs
        q = jnp.dot(h, w_ref[:, :D_MODEL], preferred_element_type=F32)
        _qk_norm_rope(q, None, cos_t * (gq_ref[...] * q_scale), sin_t * (partner_gain(gq_ref) * q_scale),
                      first_half, ones_bd, q_ref)
        k = jnp.dot(h, w_ref[:, D_MODEL:2 * D_MODEL], preferred_element_type=F32)
        _qk_norm_rope(k, None, cos_t * gk_ref[...], sin_t * partner_gain(gk_ref),
                      first_half, ones_bd, k_ref)
    else:
        k_ref, v_ref = out_refs
        k = jnp.dot(h, w_ref[:, D_MODEL:2 * D_MODEL], preferred_element_type=F32)
        _qk_norm_rope(k, gk_ref[...], None, None, None, ones_bd, k_ref)
    v_ref[0] = jnp.dot(h, w_ref[:, 2 * D_MODEL:], preferred_element_type=F32).astype(BF16)


def _qkv_call(h, w_qkv, gq, gk, cos_t, sin_t, *, rope, q_scale):
    b, t, d = h.shape
    tm = min(QKV_TOKENS, t)
    tok = pl.BlockSpec((1, tm, d), lambda i, j: (i, j, 0))
    tab = pl.BlockSpec((tm, LANES), lambda i, j: (j, 0))
    n_out = 3 if rope else 2
    return pl.pallas_call(
        functools.partial(_qkv_kernel, rope=rope, q_scale=q_scale),
        grid=(b, t // tm),
        in_specs=[tok, _resident(w_qkv.shape), _resident(gq.shape), _resident(gk.shape), tab, tab],
        out_specs=[tok] * n_out,
        out_shape=[jax.ShapeDtypeStruct((b, t, d), BF16)] * n_out,
        compiler_params=_params(2),
        name="qkv",
    )(h, w_qkv, gq, gk, cos_t, sin_t)


def _attn_kernel(q_ref, kl_ref, kc_ref, vl_ref, vc_ref, lq1_ref, lk1_ref, lq2_ref, lk2_ref,
                 subg_ref, gq_ref, gk_ref, o_ref, kcat, vext, *, lambda_init, n_ctx, q_scale):
    @pl.when(pl.program_id(2) == 0)
    def _():
        kcat[0:n_ctx, :] = kc_ref[0]
        kcat[n_ctx:, :] = kl_ref[0]
        vext[0:n_ctx, 0:V_DIM] = vc_ref[0]
        vext[n_ctx:, 0:V_DIM] = vl_ref[0]
        vext[:, V_DIM:] = jnp.ones((vext.shape[0], V_DIM), BF16)

    bound = (HEAD_DIM * q_scale * 1.05) * jnp.max(jnp.abs(gq_ref[...])) * jnp.max(jnp.abs(gk_ref[...]))
    shift_free = bound < EXP2_SAFE

    def attend(subtract_max):
        lam = (jnp.exp(jnp.sum(lq1_ref[...] * lk1_ref[...], axis=-1, keepdims=True))
               - jnp.exp(jnp.sum(lq2_ref[...] * lk2_ref[...], axis=-1, keepdims=True))
               + lambda_init)
        q = q_ref[0]
        lane = lax.broadcasted_iota(jnp.int32, q.shape, 1)
        k = kcat[...]
        v = vext[...]
        outs = []
        for comp in range(2):
            sel = (lane < HEAD_DIM) if comp == 0 else (lane >= HEAD_DIM)
            qm = jnp.where(sel, q, jnp.zeros_like(q))
            s = lax.dot_general(qm, k, (((1,), (1,)), ((), ())), preferred_element_type=F32)
            if subtract_max:
                s = s - jnp.max(s, axis=-1, keepdims=True)
            r = jnp.dot(jnp.exp2(s).astype(BF16), v, preferred_element_type=F32)
            outs.append(r[:, :V_DIM] / r[:, V_DIM:])
        o = outs[0] - lam * outs[1]
        ms = jnp.mean(o * o, axis=-1, keepdims=True)
        o_ref[0] = (o * lax.rsqrt(ms + EPS) * (subg_ref[...] * (1.0 - lambda_init))).astype(BF16)

    pl.when(shift_free)(functools.partial(attend, False))
    pl.when(jnp.logical_not(shift_free))(functools.partial(attend, True))


def _attn_call(q, k_lat, v_lat, k_ctx, v_ctx, lq1, lk1, lq2, lk2, sub_g, gq, gk, *,
               lambda_init, q_scale):
    b, s, _ = q.shape
    n_ctx = k_ctx.shape[1]
    tq = ATTN_Q_TILE
    qspec = pl.BlockSpec((1, tq, V_DIM), lambda i, h, j: (i, j, h))
    lat = pl.BlockSpec((1, s, V_DIM), lambda i, h, j: (i, 0, h))
    ctx = pl.BlockSpec((1, n_ctx, V_DIM), lambda i, h, j: (i, 0, h))
    vec = pl.BlockSpec((1, HEAD_DIM), lambda i, h, j: (0, 0))
    row = pl.BlockSpec((1, V_DIM), lambda i, h, j: (0, 0))
    return pl.pallas_call(
        functools.partial(_attn_kernel, lambda_init=lambda_init, n_ctx=n_ctx, q_scale=q_scale),
        grid=(b, N_HEADS, s // tq),
        in_specs=[qspec, lat, ctx, lat, ctx, vec, vec, vec, vec, row, row, row],
        out_specs=qspec,
        out_shape=jax.ShapeDtypeStruct(q.shape, BF16),
        scratch_shapes=[pltpu.VMEM((n_ctx + s, V_DIM), BF16),
                        pltpu.VMEM((n_ctx + s, 2 * V_DIM), BF16)],
        compiler_params=_params(3),
        name="attn",
    )(q, k_lat, k_ctx, v_lat, v_ctx, lq1, lk1, lq2, lk2, sub_g, gq, gk)


def _fourier_kernel(h_ref, w1_ref, tab_ref, o_ref, w1_bf, tab_bf, z_s):
    @pl.when((pl.program_id(0) == 0) & (pl.program_id(1) == 0))
    def _():
        w1_bf[...] = w1_ref[...].astype(BF16)
        for i in range(tab_ref.shape[0]):
            tab_bf[i] = tab_ref[i].astype(BF16)

    z = jnp.dot(h_ref[0], w1_bf[...], preferred_element_type=F32)
    for c in range(z_s.shape[0]):
        z_s[c] = z[:, c * LANES:(c + 1) * LANES]
    n2 = z_s.shape[1] // 2

    def rows(parity):
        return jnp.concatenate([z_s[c, pl.ds(parity, n2, stride=2), :].astype(BF16)
                                for c in range(z_s.shape[0])], axis=1)

    ze = rows(0)
    zo = rows(1)
    m = FOURIER_GROUP
    e = (jnp.dot(tab_bf[0], ze[:, :m], preferred_element_type=F32)
         + jnp.dot(tab_bf[1], ze[:, m:], preferred_element_type=F32))
    o = (jnp.dot(tab_bf[2], zo[:, :m], preferred_element_type=F32)
         + jnp.dot(tab_bf[3], zo[:, m:], preferred_element_type=F32))
    o_ref[0, :n2, :] = (e + o).astype(BF16)
    o_ref[0, n2:, :] = (e - o).astype(BF16)


def _fourier_call(h, w1, tab):
    b, s, d = h.shape
    blk = pl.BlockSpec((1, s, FOURIER_GROUP), lambda i, g: (i, 0, g))
    return pl.pallas_call(
        _fourier_kernel,
        grid=(b, d // FOURIER_GROUP),
        in_specs=[blk, _resident(w1.shape), _resident(tab.shape)],
        out_specs=blk,
        out_shape=jax.ShapeDtypeStruct(h.shape, BF16),
        scratch_shapes=[pltpu.VMEM(w1.shape, BF16), pltpu.VMEM(tab.shape, BF16),
                        pltpu.VMEM((2 * FOURIER_GROUP // LANES, s, LANES), F32)],
        compiler_params=_params(2),
        name="fourier",
    )(h, w1, tab)


def _dft_tables(n_pos):
    m = FOURIER_GROUP
    km = (np.outer(np.arange(m), np.arange(m)) % m) * (2.0 * np.pi / m)
    w1 = np.concatenate([np.cos(km), np.sin(km)], axis=1) / math.sqrt(m)
    k = np.arange(n_pos // 2)
    tabs = []
    for parity in range(2):
        kn = (np.outer(k, 2 * k + parity) % n_pos) * (2.0 * np.pi / n_pos)
        tabs += [np.cos(kn) / math.sqrt(n_pos), -np.sin(kn) / math.sqrt(n_pos)]
    return jnp.asarray(w1, F32), jnp.asarray(np.stack(tabs), F32)


def _rope_tables(n_pos):
    pos = np.arange(n_pos)
    inv_freq = ROPE_THETA ** (-(np.arange(ROPE_PAIRS, dtype=np.float64) / ROPE_PAIRS))
    lane = np.arange(LANES)
    axis = (lane % HEAD_DIM) // 32
    coord = np.where(axis[None, :] == 0, (pos // GRID_W)[:, None], (pos % GRID_W)[:, None])
    ang = coord * inv_freq[lane % ROPE_PAIRS][None, :]
    sign = np.where((lane % 32) < 16, -1.0, 1.0)[None, :]
    return jnp.asarray(np.cos(ang), F32), jnp.asarray(np.sin(ang) * sign, F32)


def kernel(x, c, ctx, c_ctx, norm_g, w_mod, b_mod, ffn1_w_gu, ffn1_w_d, ffn2_w_gu, ffn2_w_d,
           attn_w_qkv, attn_w_o, attn_q_g, attn_k_g, attn_lam_q1, attn_lam_k1, attn_lam_q2,
           attn_lam_k2, attn_sub_g, fourier_w):
    b, s, d = x.shape
    cond = jnp.concatenate([c, c_ctx[None, :], jnp.zeros((COND_ROWS - b - 1, d), F32)], axis=0)
    mods = _ada_call(cond, w_mod, b_mod)
    mx = [mods[i, :b].reshape(b, N_MOD, d) for i in range(2)]
    my0 = mods[0, b:b + 1].reshape(1, N_MOD, d)

    wgu1 = [ffn1_w_gu[i].astype(BF16) for i in range(2)]
    wd1 = [ffn1_w_d[i].astype(BF16) for i in range(2)]
    wgu2 = [ffn2_w_gu[i].astype(BF16) for i in range(2)]
    wd2 = [ffn2_w_d[i].astype(BF16) for i in range(2)]
    w_qkv = attn_w_qkv[0].astype(BF16)
    w_o = attn_w_o[0].astype(BF16)
    w_f = fourier_w[0].astype(BF16)

    x, hx = _ffn_call(x, mx[0], norm_g[0], wgu1[0], wd1[0], ffn_idx=0, emit_h=True)
    _, hy = _ffn_call(ctx, my0, norm_g[0], wgu1[0], wd1[0], ffn_idx=0, emit_h=True)
    cos_t, sin_t = _rope_tables(s)
    gq = jnp.tile(attn_q_g[0], 2)[None, :]
    gk = jnp.tile(attn_k_g[0], 2)[None, :]
    q_scale = math.log2(math.e) / math.sqrt(HEAD_DIM)
    q, k, v = _qkv_call(hx, w_qkv, gq, gk, cos_t, sin_t, rope=True, q_scale=q_scale)
    ky, vy = _qkv_call(hy, w_qkv, gq, gk, cos_t, sin_t, rope=False, q_scale=1.0)
    lambda_init = 0.8 - 0.6 * math.exp(-0.3 * 0)
    o = _attn_call(q, k, v, ky, vy, attn_lam_q1[0:1], attn_lam_k1[0:1], attn_lam_q2[0:1],
                   attn_lam_k2[0:1], attn_sub_g[0:1], gq, gk, lambda_init=lambda_init,
                   q_scale=q_scale)
    x = _ffn_call(x, mx[0], norm_g[0], wgu2[0], wd2[0], ffn_idx=1, mix=(o, w_o))

    x, hx = _ffn_call(x, mx[1], norm_g[1], wgu1[1], wd1[1], ffn_idx=0, emit_h=True)
    f = _fourier_call(hx, *_dft_tables(s))
    x = _ffn_call(x, mx[1], norm_g[1], wgu2[1], wd2[1], ffn_idx=1, mix=(f, w_f))
    return x
```

```python
import functools
import math

import numpy as np
import jax
import jax.numpy as jnp
from jax import lax
from jax.experimental import pallas as pl
from jax.experimental.pallas import tpu as pltpu

F32 = jnp.float32
BF16 = jnp.bfloat16

D_MODEL = 1024
N_MOD = 9
N_HEADS = 8
HEAD_DIM = 64
V_DIM = 2 * HEAD_DIM
GRID_W = 64
ROPE_THETA = 10000.0
ROPE_PAIRS = HEAD_DIM // 4
N_FOURIER_GROUPS = 4
FOURIER_GROUP = D_MODEL // N_FOURIER_GROUPS
D_FF = 2816
EPS = 1e-6
EXP2_SAFE = 100.0

LANES = 128
MXU_DIM = 256
COND_ROWS = 24
VMEM_LIMIT = 56 * 1024 * 1024

FFN_TOKENS = 1024
FFN_SUBTILE = 512
FFN_CHUNK = 256
QKV_TOKENS = 1024
ATTN_Q_TILE = 1024
ADA_COLS = 1024


def _resident(shape):
    nd = len(shape)
    return pl.BlockSpec(shape, lambda *_: (0,) * nd, pipeline_mode=pl.Buffered(1))


def _params(n_axes):
    return pltpu.CompilerParams(dimension_semantics=("arbitrary",) * n_axes,
                                vmem_limit_bytes=VMEM_LIMIT)


def _ada_kernel(cond_ref, w_ref, b_ref, o_ref):
    cnd = cond_ref[...]
    act = (cnd * jax.nn.sigmoid(cnd)).astype(BF16)
    o_ref[0] = jnp.dot(act, w_ref[0].astype(BF16), preferred_element_type=F32) + b_ref[0]


def _ada_call(cond, w_mod, b_mod):
    depth, _, width = w_mod.shape
    return pl.pallas_call(
        _ada_kernel,
        grid=(depth, width // ADA_COLS),
        in_specs=[
            pl.BlockSpec((COND_ROWS, D_MODEL), lambda i, j: (0, 0)),
            pl.BlockSpec((1, D_MODEL, ADA_COLS), lambda i, j: (i, 0, j)),
            pl.BlockSpec((1, 1, ADA_COLS), lambda i, j: (i, 0, j)),
        ],
        out_specs=pl.BlockSpec((1, COND_ROWS, ADA_COLS), lambda i, j: (i, 0, j)),
        out_shape=jax.ShapeDtypeStruct((depth, COND_ROWS, width), F32),
        compiler_params=_params(2),
        name="ada",
    )(cond, w_mod, b_mod.reshape(depth, 1, width))


def _rms_mod(x, gain, shift, scale):
    ms = jnp.mean(x * x, axis=-1, keepdims=True)
    return x * lax.rsqrt(ms + EPS) * (gain * (1.0 + scale)) + shift


def _ffn_kernel(*refs, premix, emit_h, ffn_idx):
    x_ref, mod_ref, gain_ref = refs[:3]
    refs = refs[3:]
    if premix:
        t_ref, wt_ref = refs[:2]
        refs = refs[2:]
    wgu_ref, wd_ref, out_ref = refs[:3]
    ks = 6 if ffn_idx else 0
    grow = 2 if ffn_idx else 0
    sub = min(FFN_SUBTILE, x_ref.shape[1])
    for st in range(x_ref.shape[1] // sub):
        rows = slice(st * sub, (st + 1) * sub)
        x = x_ref[0, rows, :]
        if premix:
            x = x + mod_ref[0, 5:6, :] * jnp.dot(t_ref[0, rows, :], wt_ref[...],
                                                 preferred_element_type=F32)
        h = _rms_mod(x, gain_ref[grow:grow + 1, :], mod_ref[0, ks:ks + 1, :],
                     mod_ref[0, ks + 1:ks + 2, :]).astype(BF16)
        acc = jnp.zeros(x.shape, F32)
        for c in range(D_FF // FFN_CHUNK):
            lo = c * FFN_CHUNK
            g = jnp.dot(h, wgu_ref[:, lo:lo + FFN_CHUNK], preferred_element_type=F32)
            u = jnp.dot(h, wgu_ref[:, D_FF + lo:D_FF + lo + FFN_CHUNK], preferred_element_type=F32)
            a = (g * jax.nn.sigmoid(g) * u).astype(BF16)
            acc = acc + jnp.dot(a, wd_ref[lo:lo + FFN_CHUNK, :], preferred_element_type=F32)
        out = x + (0.5 * mod_ref[0, ks + 2:ks + 3, :]) * acc
        out_ref[0, rows, :] = out
        if emit_h:
            refs[3][0, rows, :] = _rms_mod(out, gain_ref[1:2, :], mod_ref[0, 3:4, :],
                                           mod_ref[0, 4:5, :]).astype(BF16)


def _resident_layer(shape, layer):
    tail = (0,) * (len(shape) - 1)
    return pl.BlockSpec((None,) + tuple(shape[1:]), lambda *_: (layer,) + tail,
                        pipeline_mode=pl.Buffered(1))


def _ffn_call(x, mod, gains, w_gu, w_d, layer, *, ffn_idx, mix=None, emit_h=False):
    b, t, d = x.shape
    tm = min(FFN_TOKENS, t)
    tok = pl.BlockSpec((1, tm, d), lambda i, j: (i, j, 0))
    mod_map = (lambda i, j: (i, 0, 0)) if mod.shape[0] == b else (lambda i, j: (0, 0, 0))
    in_specs = [tok, pl.BlockSpec((1, N_MOD, d), mod_map), _resident(gains.shape)]
    args = [x, mod, gains]
    if mix is not None:
        in_specs += [tok, _resident(mix[1].shape)]
        args += [mix[0], mix[1]]
    in_specs += [_resident_layer(w_gu.shape, layer), _resident_layer(w_d.shape, layer)]
    args += [w_gu, w_d]
    out_shape = [jax.ShapeDtypeStruct(x.shape, F32)]
    out_specs = [tok]
    if emit_h:
        out_shape.append(jax.ShapeDtypeStruct(x.shape, BF16))
        out_specs.append(tok)
    res = pl.pallas_call(
        functools.partial(_ffn_kernel, premix=mix is not None, emit_h=emit_h, ffn_idx=ffn_idx),
        grid=(b, t // tm),
        in_specs=in_specs,
        out_specs=out_specs,
        out_shape=out_shape,
        compiler_params=_params(2),
        name="ffn",
    )(*args)
    return res if emit_h else res[0]


def _group_ones():
    r = lax.broadcasted_iota(jnp.int32, (MXU_DIM, MXU_DIM), 0) // HEAD_DIM
    c = lax.broadcasted_iota(jnp.int32, (MXU_DIM, MXU_DIM), 1) // HEAD_DIM
    return (r == c).astype(BF16)


def _qk_norm_rope(t, gain, cos_g, sin_g, first_half, ones_bd, out_ref):
    sq = (t * t).astype(BF16)
    for j in range(t.shape[1] // MXU_DIM):
        ssq = jnp.dot(sq[:, j * MXU_DIM:(j + 1) * MXU_DIM], ones_bd, preferred_element_type=F32)
        inv = lax.rsqrt(ssq * (1.0 / HEAD_DIM) + EPS)
        for jj in range(MXU_DIM // LANES):
            lo = j * MXU_DIM + jj * LANES
            tb = t[:, lo:lo + LANES]
            ib = inv[:, jj * LANES:(jj + 1) * LANES]
            if cos_g is None:
                ob = tb * ib * gain
            else:
                partner = jnp.where(first_half, pltpu.roll(tb, LANES - 16, 1), pltpu.roll(tb, 16, 1))
                ob = ib * (tb * cos_g + partner * sin_g)
            out_ref[0, :, lo:lo + LANES] = ob.astype(BF16)


def _qkv_kernel(h_ref, w_ref, gq_ref, gk_ref, cos_ref, sin_ref, *out_refs, rope, q_scale):
    h = h_ref[0]
    ones_bd = _group_ones()
    first_half = None
    if rope:
        lane = lax.broadcasted_iota(jnp.int32, (h.shape[0], LANES), 1)
        first_half = (lane % 32) < 16
        cos_t = cos_ref[...]
        sin_t = sin_ref[...]

        def partner_gain(g_ref):
            g8 = jnp.broadcast_to(g_ref[...], (8, LANES))
            fh = (lax.broadcasted_iota(jnp.int32, (8, LANES), 1) % 32) < 16
            return jnp.where(fh, pltpu.roll(g8, LANES - 16, 1), pltpu.roll(g8, 16, 1))[:1]

        q_ref, k_ref, v_ref = out_refs
        q = jnp.dot(h, w_ref[:, :D_MODEL], preferred_element_type=F32)
        _qk_norm_rope(q, None, cos_t * (gq_ref[...] * q_scale), sin_t * (partner_gain(gq_ref) * q_scale),
                      first_half, ones_bd, q_ref)
        k = jnp.dot(h, w_ref[:, D_MODEL:2 * D_MODEL], preferred_element_type=F32)
        _qk_norm_rope(k, None, cos_t * gk_ref[...], sin_t * partner_gain(gk_ref),
                      first_half, ones_bd, k_ref)
    else:
        k_ref, v_ref = out_refs
        k = jnp.dot(h, w_ref[:, D_MODEL:2 * D_MODEL], preferred_element_type=F32)
        _qk_norm_rope(k, gk_ref[...], None, None, None, ones_bd, k_ref)
    v_ref[0] = jnp.dot(h, w_ref[:, 2 * D_MODEL:], preferred_element_type=F32).astype(BF16)


def _qkv_call(h, w_qkv, gq, gk, cos_t, sin_t, *, rope, q_scale):
    b, t, d = h.shape
    tm = min(QKV_TOKENS, t)
    tok = pl.BlockSpec((1, tm, d), lambda i, j: (i, j, 0))
    tab = pl.BlockSpec((tm, LANES), lambda i, j: (j, 0))
    n_out = 3 if rope else 2
    return pl.pallas_call(
        functools.partial(_qkv_kernel, rope=rope, q_scale=q_scale),
        grid=(b, t // tm),
        in_specs=[tok, _resident(w_qkv.shape), _resident(gq.shape), _resident(gk.shape), tab, tab],
        out_specs=[tok] * n_out,
        out_shape=[jax.ShapeDtypeStruct((b, t, d), BF16)] * n_out,
        compiler_params=_params(2),
        name="qkv",
    )(h, w_qkv, gq, gk, cos_t, sin_t)


def _attn_kernel(q_ref, kl_ref, kc_ref, vl_ref, vc_ref, lq1_ref, lk1_ref, lq2_ref, lk2_ref,
                 subg_ref, gq_ref, gk_ref, o_ref, kcat, vext, *, lambda_init, n_ctx, q_scale):
    @pl.when(pl.program_id(2) == 0)
    def _():
        kcat[0:n_ctx, :] = kc_ref[0]
        kcat[n_ctx:, :] = kl_ref[0]
        vext[0:n_ctx, 0:V_DIM] = vc_ref[0]
        vext[n_ctx:, 0:V_DIM] = vl_ref[0]
        vext[:, V_DIM:] = jnp.ones((vext.shape[0], V_DIM), BF16)

    bound = (HEAD_DIM * q_scale * 1.05) * jnp.max(jnp.abs(gq_ref[...])) * jnp.max(jnp.abs(gk_ref[...]))
    shift_free = bound < EXP2_SAFE

    def scores(rows, comp):
        q = q_ref[0, rows, :]
        lane = lax.broadcasted_iota(jnp.int32, q.shape, 1)
        sel = (lane < HEAD_DIM) if comp == 0 else (lane >= HEAD_DIM)
        qm = jnp.where(sel, q, jnp.zeros_like(q))
        return lax.dot_general(qm, kcat[...], (((1,), (1,)), ((), ())), preferred_element_type=F32)

    def finish(rows, o):
        ms = jnp.mean(o * o, axis=-1, keepdims=True)
        o_ref[0, rows, :] = (o * lax.rsqrt(ms + EPS)
                             * (subg_ref[...] * (1.0 - lambda_init))).astype(BF16)

    def lam_value():
        return (jnp.exp(jnp.sum(lq1_ref[...] * lk1_ref[...], axis=-1, keepdims=True))
                - jnp.exp(jnp.sum(lq2_ref[...] * lk2_ref[...], axis=-1, keepdims=True))
                + lambda_init)

    def attend(subtract_max):
        lam = lam_value()
        rows = slice(0, q_ref.shape[1])
        outs = []
        for comp in range(2):
            s = scores(rows, comp)
            if subtract_max:
                s = s - jnp.max(s, axis=-1, keepdims=True)
            r = jnp.dot(jnp.exp2(s).astype(BF16), vext[...], preferred_element_type=F32)
            outs.append(r[:, :V_DIM] / r[:, V_DIM:])
        finish(rows, outs[0] - lam * outs[1])

    pl.when(shift_free)(functools.partial(attend, False))
    pl.when(jnp.logical_not(shift_free))(functools.partial(attend, True))


def _attn_call(q, k_lat, v_lat, k_ctx, v_ctx, lq1, lk1, lq2, lk2, sub_g, gq, gk, *,
               lambda_init, q_scale):
    b, s, _ = q.shape
    n_ctx = k_ctx.shape[1]
    tq = ATTN_Q_TILE
    qspec = pl.BlockSpec((1, tq, V_DIM), lambda i, h, j: (i, j, h))
    lat = pl.BlockSpec((1, s, V_DIM), lambda i, h, j: (i, 0, h))
    ctx = pl.BlockSpec((1, n_ctx, V_DIM), lambda i, h, j: (i, 0, h))
    vec = pl.BlockSpec((1, HEAD_DIM), lambda i, h, j: (0, 0))
    row = pl.BlockSpec((1, V_DIM), lambda i, h, j: (0, 0))
    return pl.pallas_call(
        functools.partial(_attn_kernel, lambda_init=lambda_init, n_ctx=n_ctx, q_scale=q_scale),
        grid=(b, N_HEADS, s // tq),
        in_specs=[qspec, lat, ctx, lat, ctx, vec, vec, vec, vec, row, row, row],
        out_specs=qspec,
        out_shape=jax.ShapeDtypeStruct(q.shape, BF16),
        scratch_shapes=[pltpu.VMEM((n_ctx + s, V_DIM), BF16),
                        pltpu.VMEM((n_ctx + s, 2 * V_DIM), BF16)],
        compiler_params=_params(3),
        name="attn",
    )(q, k_lat, k_ctx, v_lat, v_ctx, lq1, lk1, lq2, lk2, sub_g, gq, gk)


def _fourier_kernel(h_ref, w1_ref, tab_ref, o_ref, w1_bf, tab_bf, z_s):
    @pl.when((pl.program_id(0) == 0) & (pl.program_id(1) == 0))
    def _():
        w1_bf[...] = w1_ref[...].astype(BF16)
        for i in range(tab_ref.shape[0]):
            tab_bf[i] = tab_ref[i].astype(BF16)

    z = jnp.dot(h_ref[0], w1_bf[...], preferred_element_type=F32)
    for c in range(z_s.shape[0]):
        z_s[c] = z[:, c * LANES:(c + 1) * LANES]
    n2 = z_s.shape[1] // 2

    def rows(parity):
        return jnp.concatenate([z_s[c, pl.ds(parity, n2, stride=2), :].astype(BF16)
                                for c in range(z_s.shape[0])], axis=1)

    ze = rows(0)
    zo = rows(1)
    m = FOURIER_GROUP
    e = (jnp.dot(tab_bf[0], ze[:, :m], preferred_element_type=F32)
         + jnp.dot(tab_bf[1], ze[:, m:], preferred_element_type=F32))
    o = (jnp.dot(tab_bf[2], zo[:, :m], preferred_element_type=F32)
         + jnp.dot(tab_bf[3], zo[:, m:], preferred_element_type=F32))
    o_ref[0, :n2, :] = (e + o).astype(BF16)
    o_ref[0, n2:, :] = (e - o).astype(BF16)


def _fourier_call(h, w1, tab):
    b, s, d = h.shape
    blk = pl.BlockSpec((1, s, FOURIER_GROUP), lambda i, g: (i, 0, g))
    return pl.pallas_call(
        _fourier_kernel,
        grid=(b, d // FOURIER_GROUP),
        in_specs=[blk, _resident(w1.shape), _resident(tab.shape)],
        out_specs=blk,
        out_shape=jax.ShapeDtypeStruct(h.shape, BF16),
        scratch_shapes=[pltpu.VMEM(w1.shape, BF16), pltpu.VMEM(tab.shape, BF16),
                        pltpu.VMEM((2 * FOURIER_GROUP // LANES, s, LANES), F32)],
        compiler_params=_params(2),
        name="fourier",
    )(h, w1, tab)


def _dft_tables(n_pos):
    m = FOURIER_GROUP
    km = (np.outer(np.arange(m), np.arange(m)) % m) * (2.0 * np.pi / m)
    w1 = np.concatenate([np.cos(km), np.sin(km)], axis=1) / math.sqrt(m)
    k = np.arange(n_pos // 2)
    tabs = []
    for parity in range(2):
        kn = (np.outer(k, 2 * k + parity) % n_pos) * (2.0 * np.pi / n_pos)
        tabs += [np.cos(kn) / math.sqrt(n_pos), -np.sin(kn) / math.sqrt(n_pos)]
    return jnp.asarray(w1, F32), jnp.asarray(np.stack(tabs), F32)


def _rope_tables(n_pos):
    pos = np.arange(n_pos)
    inv_freq = ROPE_THETA ** (-(np.arange(ROPE_PAIRS, dtype=np.float64) / ROPE_PAIRS))
    lane = np.arange(LANES)
    axis = (lane % HEAD_DIM) // 32
    coord = np.where(axis[None, :] == 0, (pos // GRID_W)[:, None], (pos % GRID_W)[:, None])
    ang = coord * inv_freq[lane % ROPE_PAIRS][None, :]
    sign = np.where((lane % 32) < 16, -1.0, 1.0)[None, :]
    return jnp.asarray(np.cos(ang), F32), jnp.asarray(np.sin(ang) * sign, F32)


def kernel(x, c, ctx, c_ctx, norm_g, w_mod, b_mod, ffn1_w_gu, ffn1_w_d, ffn2_w_gu, ffn2_w_d,
           attn_w_qkv, attn_w_o, attn_q_g, attn_k_g, attn_lam_q1, attn_lam_k1, attn_lam_q2,
           attn_lam_k2, attn_sub_g, fourier_w):
    b, s, d = x.shape
    cond = jnp.concatenate([c, c_ctx[None, :], jnp.zeros((COND_ROWS - b - 1, d), F32)], axis=0)
    mods = _ada_call(cond, w_mod, b_mod)
    mx = [mods[i, :b].reshape(b, N_MOD, d) for i in range(2)]
    my0 = mods[0, b:b + 1].reshape(1, N_MOD, d)

    wgu1, wd1 = ffn1_w_gu.astype(BF16), ffn1_w_d.astype(BF16)
    wgu2, wd2 = ffn2_w_gu.astype(BF16), ffn2_w_d.astype(BF16)
    w_qkv = attn_w_qkv[0].astype(BF16)
    w_o = attn_w_o[0].astype(BF16)
    w_f = fourier_w[0].astype(BF16)

    x, hx = _ffn_call(x, mx[0], norm_g[0], wgu1, wd1, 0, ffn_idx=0, emit_h=True)
    _, hy = _ffn_call(ctx, my0, norm_g[0], wgu1, wd1, 0, ffn_idx=0, emit_h=True)
    cos_t, sin_t = _rope_tables(s)
    gq = jnp.tile(attn_q_g[0], 2)[None, :]
    gk = jnp.tile(attn_k_g[0], 2)[None, :]
    q_scale = math.log2(math.e) / math.sqrt(HEAD_DIM)
    q, k, v = _qkv_call(hx, w_qkv, gq, gk, cos_t, sin_t, rope=True, q_scale=q_scale)
    ky, vy = _qkv_call(hy, w_qkv, gq, gk, cos_t, sin_t, rope=False, q_scale=1.0)
    lambda_init = 0.8 - 0.6 * math.exp(-0.3 * 0)
    o = _attn_call(q, k, v, ky, vy, attn_lam_q1[0:1], attn_lam_k1[0:1], attn_lam_q2[0:1],
                   attn_lam_k2[0:1], attn_sub_g[0:1], gq, gk, lambda_init=lambda_init,
                   q_scale=q_scale)
    x = _ffn_call(x, mx[0], norm_g[0], wgu2, wd2, 0, ffn_idx=1, mix=(o, w_o))

    x, hx = _ffn_call(x, mx[1], norm_g[1], wgu1, wd1, 1, ffn_idx=0, emit_h=True)
    f = _fourier_call(hx, *_dft_tables(s))
    x = _ffn_call(x, mx[1], norm_g[1], wgu2, wd2, 1, ffn_idx=1, mix=(f, w_f))
    return x
```

```python
import functools
import math

import numpy as np
import jax
import jax.numpy as jnp
from jax import lax
from jax.experimental import pallas as pl
from jax.experimental.pallas import tpu as pltpu

F32 = jnp.float32
BF16 = jnp.bfloat16

D_MODEL = 1024
N_MOD = 9
N_HEADS = 8
HEAD_DIM = 64
V_DIM = 2 * HEAD_DIM
GRID_W = 64
ROPE_THETA = 10000.0
ROPE_PAIRS = HEAD_DIM // 4
N_FOURIER_GROUPS = 4
FOURIER_GROUP = D_MODEL // N_FOURIER_GROUPS
D_FF = 2816
EPS = 1e-6
EXP2_SAFE = 100.0

LANES = 128
MXU_DIM = 256
COND_ROWS = 24
VMEM_LIMIT = 56 * 1024 * 1024

FFN_TOKENS = 1024
FFN_SUBTILE = 512
FFN_CHUNK = 256
QKV_TOKENS = 1024
ATTN_Q_TILE = 1024
ONES_ROWS = 16
ADA_COLS = 1024


def _resident(shape):
    nd = len(shape)
    return pl.BlockSpec(shape, lambda *_: (0,) * nd, pipeline_mode=pl.Buffered(1))


def _params(n_axes):
    return pltpu.CompilerParams(dimension_semantics=("arbitrary",) * n_axes,
                                vmem_limit_bytes=VMEM_LIMIT)


def _ada_kernel(cond_ref, w_ref, b_ref, o_ref):
    cnd = cond_ref[...]
    act = (cnd * jax.nn.sigmoid(cnd)).astype(BF16)
    o_ref[0] = jnp.dot(act, w_ref[0].astype(BF16), preferred_element_type=F32) + b_ref[0]


def _ada_call(cond, w_mod, b_mod):
    depth, _, width = w_mod.shape
    return pl.pallas_call(
        _ada_kernel,
        grid=(depth, width // ADA_COLS),
        in_specs=[
            pl.BlockSpec((COND_ROWS, D_MODEL), lambda i, j: (0, 0)),
            pl.BlockSpec((1, D_MODEL, ADA_COLS), lambda i, j: (i, 0, j)),
            pl.BlockSpec((1, 1, ADA_COLS), lambda i, j: (i, 0, j)),
        ],
        out_specs=pl.BlockSpec((1, COND_ROWS, ADA_COLS), lambda i, j: (i, 0, j)),
        out_shape=jax.ShapeDtypeStruct((depth, COND_ROWS, width), F32),
        compiler_params=_params(2),
        name="ada",
    )(cond, w_mod, b_mod.reshape(depth, 1, width))


def _rms_mod(x, gain, shift, scale):
    ms = jnp.mean(x * x, axis=-1, keepdims=True)
    return x * lax.rsqrt(ms + EPS) * (gain * (1.0 + scale)) + shift


def _ffn_kernel(*refs, premix, emit_h, ffn_idx):
    x_ref, mod_ref, gain_ref = refs[:3]
    refs = refs[3:]
    if premix:
        t_ref, wt_ref = refs[:2]
        refs = refs[2:]
    wgu_ref, wd_ref, out_ref = refs[:3]
    ks = 6 if ffn_idx else 0
    grow = 2 if ffn_idx else 0
    sub = min(FFN_SUBTILE, x_ref.shape[1])
    for st in range(x_ref.shape[1] // sub):
        rows = slice(st * sub, (st + 1) * sub)
        x = x_ref[0, rows, :]
        if premix:
            x = x + mod_ref[0, 5:6, :] * jnp.dot(t_ref[0, rows, :], wt_ref[...],
                                                 preferred_element_type=F32)
        h = _rms_mod(x, gain_ref[grow:grow + 1, :], mod_ref[0, ks:ks + 1, :],
                     mod_ref[0, ks + 1:ks + 2, :]).astype(BF16)
        acc = jnp.zeros(x.shape, F32)
        for c in range(D_FF // FFN_CHUNK):
            lo = c * FFN_CHUNK
            g = jnp.dot(h, wgu_ref[:, lo:lo + FFN_CHUNK], preferred_element_type=F32)
            u = jnp.dot(h, wgu_ref[:, D_FF + lo:D_FF + lo + FFN_CHUNK], preferred_element_type=F32)
            a = (g * jax.nn.sigmoid(g) * u).astype(BF16)
            acc = acc + jnp.dot(a, wd_ref[lo:lo + FFN_CHUNK, :], preferred_element_type=F32)
        out = x + (0.5 * mod_ref[0, ks + 2:ks + 3, :]) * acc
        out_ref[0, rows, :] = out
        if emit_h:
            refs[3][0, rows, :] = _rms_mod(out, gain_ref[1:2, :], mod_ref[0, 3:4, :],
                                           mod_ref[0, 4:5, :]).astype(BF16)


def _resident_layer(shape, layer):
    tail = (0,) * (len(shape) - 1)
    return pl.BlockSpec((None,) + tuple(shape[1:]), lambda *_: (layer,) + tail,
                        pipeline_mode=pl.Buffered(1))


def _ffn_call(x, mod, gains, w_gu, w_d, layer, *, ffn_idx, mix=None, emit_h=False):
    b, t, d = x.shape
    tm = min(FFN_TOKENS, t)
    tok = pl.BlockSpec((1, tm, d), lambda i, j: (i, j, 0))
    mod_map = (lambda i, j: (i, 0, 0)) if mod.shape[0] == b else (lambda i, j: (0, 0, 0))
    in_specs = [tok, pl.BlockSpec((1, N_MOD, d), mod_map), _resident(gains.shape)]
    args = [x, mod, gains]
    if mix is not None:
        in_specs += [tok, _resident(mix[1].shape)]
        args += [mix[0], mix[1]]
    in_specs += [_resident_layer(w_gu.shape, layer), _resident_layer(w_d.shape, layer)]
    args += [w_gu, w_d]
    out_shape = [jax.ShapeDtypeStruct(x.shape, F32)]
    out_specs = [tok]
    if emit_h:
        out_shape.append(jax.ShapeDtypeStruct(x.shape, BF16))
        out_specs.append(tok)
    res = pl.pallas_call(
        functools.partial(_ffn_kernel, premix=mix is not None, emit_h=emit_h, ffn_idx=ffn_idx),
        grid=(b, t // tm),
        in_specs=in_specs,
        out_specs=out_specs,
        out_shape=out_shape,
        compiler_params=_params(2),
        name="ffn",
    )(*args)
    return res if emit_h else res[0]


def _group_ones():
    r = lax.broadcasted_iota(jnp.int32, (MXU_DIM, MXU_DIM), 0) // HEAD_DIM
    c = lax.broadcasted_iota(jnp.int32, (MXU_DIM, MXU_DIM), 1) // HEAD_DIM
    return (r == c).astype(BF16)


def _qk_norm_rope(t, gain, cos_g, sin_g, first_half, ones_bd, out_ref):
    sq = (t * t).astype(BF16)
    for j in range(t.shape[1] // MXU_DIM):
        ssq = jnp.dot(sq[:, j * MXU_DIM:(j + 1) * MXU_DIM], ones_bd, preferred_element_type=F32)
        inv = lax.rsqrt(ssq * (1.0 / HEAD_DIM) + EPS)
        for jj in range(MXU_DIM // LANES):
            lo = j * MXU_DIM + jj * LANES
            tb = t[:, lo:lo + LANES]
            ib = inv[:, jj * LANES:(jj + 1) * LANES]
            if cos_g is None:
                ob = tb * ib * gain
            else:
                partner = jnp.where(first_half, pltpu.roll(tb, LANES - 16, 1), pltpu.roll(tb, 16, 1))
                ob = ib * (tb * cos_g + partner * sin_g)
            out_ref[0, :, lo:lo + LANES] = ob.astype(BF16)


def _qkv_kernel(h_ref, w_ref, gq_ref, gk_ref, cos_ref, sin_ref, *out_refs, rope, q_scale):
    h = h_ref[0]
    ones_bd = _group_ones()
    first_half = None
    if rope:
        lane = lax.broadcasted_iota(jnp.int32, (h.shape[0], LANES), 1)
        first_half = (lane % 32) < 16
        cos_t = cos_ref[...]
        sin_t = sin_ref[...]

        def partner_gain(g_ref):
            g8 = jnp.broadcast_to(g_ref[...], (8, LANES))
            fh = (lax.broadcasted_iota(jnp.int32, (8, LANES), 1) % 32) < 16
            return jnp.where(fh, pltpu.roll(g8, LANES - 16, 1), pltpu.roll(g8, 16, 1))[:1]

        q_ref, k_ref, v_ref = out_refs
        q = jnp.dot(h, w_ref[:, :D_MODEL], preferred_element_type=F32)
        _qk_norm_rope(q, None, cos_t * (gq_ref[...] * q_scale), sin_t * (partner_gain(gq_ref) * q_scale),
                      first_half, ones_bd, q_ref)
        k = jnp.dot(h, w_ref[:, D_MODEL:2 * D_MODEL], preferred_element_type=F32)
        _qk_norm_rope(k, None, cos_t * gk_ref[...], sin_t * partner_gain(gk_ref),
                      first_half, ones_bd, k_ref)
    else:
        k_ref, v_ref = out_refs
        k = jnp.dot(h, w_ref[:, D_MODEL:2 * D_MODEL], preferred_element_type=F32)
        _qk_norm_rope(k, gk_ref[...], None, None, None, ones_bd, k_ref)
    v_ref[0] = jnp.dot(h, w_ref[:, 2 * D_MODEL:], preferred_element_type=F32).astype(BF16)


def _qkv_call(h, w_qkv, gq, gk, cos_t, sin_t, *, rope, q_scale):
    b, t, d = h.shape
    tm = min(QKV_TOKENS, t)
    tok = pl.BlockSpec((1, tm, d), lambda i, j: (i, j, 0))
    tab = pl.BlockSpec((tm, LANES), lambda i, j: (j, 0))
    n_out = 3 if rope else 2
    return pl.pallas_call(
        functools.partial(_qkv_kernel, rope=rope, q_scale=q_scale),
        grid=(b, t // tm),
        in_specs=[tok, _resident(w_qkv.shape), _resident(gq.shape), _resident(gk.shape), tab, tab],
        out_specs=[tok] * n_out,
        out_shape=[jax.ShapeDtypeStruct((b, t, d), BF16)] * n_out,
        compiler_params=_params(2),
        name="qkv",
    )(h, w_qkv, gq, gk, cos_t, sin_t)


def _attn_kernel(q_ref, kl_ref, kc_ref, vl_ref, vc_ref, lq1_ref, lk1_ref, lq2_ref, lk2_ref,
                 subg_ref, gq_ref, gk_ref, o_ref, kcat, vt, *, lambda_init, n_ctx, q_scale):
    @pl.when(pl.program_id(2) == 0)
    def _():
        kcat[0:n_ctx, :] = kc_ref[0]
        kcat[n_ctx:, :] = kl_ref[0]
        vt[0:V_DIM, 0:n_ctx] = vc_ref[0].T
        vt[0:V_DIM, n_ctx:] = vl_ref[0].T
        vt[V_DIM:, :] = jnp.ones((vt.shape[0] - V_DIM, vt.shape[1]), BF16)

    bound = (HEAD_DIM * q_scale * 1.05) * jnp.max(jnp.abs(gq_ref[...])) * jnp.max(jnp.abs(gk_ref[...]))
    shift_free = bound < EXP2_SAFE

    def attend(subtract_max):
        lam = (jnp.exp(jnp.sum(lq1_ref[...] * lk1_ref[...], axis=-1, keepdims=True))
               - jnp.exp(jnp.sum(lq2_ref[...] * lk2_ref[...], axis=-1, keepdims=True))
               + lambda_init)
        q = q_ref[0]
        lane = lax.broadcasted_iota(jnp.int32, q.shape, 1)
        outs = []
        for comp in range(2):
            sel = (lane < HEAD_DIM) if comp == 0 else (lane >= HEAD_DIM)
            qm = jnp.where(sel, q, jnp.zeros_like(q))
            s = lax.dot_general(kcat[...], qm, (((1,), (1,)), ((), ())),
                                preferred_element_type=F32)
            if subtract_max:
                s = s - jnp.max(s, axis=0, keepdims=True)
            r = jnp.dot(vt[...], jnp.exp2(s).astype(BF16), preferred_element_type=F32)
            outs.append(r[:V_DIM] / r[V_DIM:V_DIM + 1])
        o = outs[0] - lam * outs[1]
        ms = jnp.mean(o * o, axis=0, keepdims=True)
        o = o * lax.rsqrt(ms + EPS) * (subg_ref[...] * (1.0 - lambda_init))
        o_ref[0] = o.T.astype(BF16)

    pl.when(shift_free)(functools.partial(attend, False))
    pl.when(jnp.logical_not(shift_free))(functools.partial(attend, True))


def _attn_call(q, k_lat, v_lat, k_ctx, v_ctx, lq1, lk1, lq2, lk2, sub_g, gq, gk, *,
               lambda_init, q_scale):
    b, s, _ = q.shape
    n_ctx = k_ctx.shape[1]
    tq = ATTN_Q_TILE
    qspec = pl.BlockSpec((1, tq, V_DIM), lambda i, h, j: (i, j, h))
    lat = pl.BlockSpec((1, s, V_DIM), lambda i, h, j: (i, 0, h))
    ctx = pl.BlockSpec((1, n_ctx, V_DIM), lambda i, h, j: (i, 0, h))
    vec = pl.BlockSpec((1, HEAD_DIM), lambda i, h, j: (0, 0))
    row = pl.BlockSpec((1, V_DIM), lambda i, h, j: (0, 0))
    return pl.pallas_call(
        functools.partial(_attn_kernel, lambda_init=lambda_init, n_ctx=n_ctx, q_scale=q_scale),
        grid=(b, N_HEADS, s // tq),
        in_specs=[qspec, lat, ctx, lat, ctx, vec, vec, vec, vec,
                  pl.BlockSpec((V_DIM, 1), lambda i, h, j: (0, 0)), row, row],
        out_specs=qspec,
        out_shape=jax.ShapeDtypeStruct(q.shape, BF16),
        scratch_shapes=[pltpu.VMEM((n_ctx + s, V_DIM), BF16),
                        pltpu.VMEM((V_DIM + ONES_ROWS, n_ctx + s), BF16)],
        compiler_params=_params(3),
        name="attn",
    )(q, k_lat, k_ctx, v_lat, v_ctx, lq1, lk1, lq2, lk2, sub_g.reshape(V_DIM, 1), gq, gk)


def _fourier_kernel(h_ref, w1_ref, tab_ref, o_ref, w1_bf, tab_bf, z_s):
    @pl.when((pl.program_id(0) == 0) & (pl.program_id(1) == 0))
    def _():
        w1_bf[...] = w1_ref[...].astype(BF16)
        for i in range(tab_ref.shape[0]):
            tab_bf[i] = tab_ref[i].astype(BF16)

    z = jnp.dot(h_ref[0], w1_bf[...], preferred_element_type=F32)
    for c in range(z_s.shape[0]):
        z_s[c] = z[:, c * LANES:(c + 1) * LANES]
    n2 = z_s.shape[1] // 2

    def rows(parity):
        return jnp.concatenate([z_s[c, pl.ds(parity, n2, stride=2), :].astype(BF16)
                                for c in range(z_s.shape[0])], axis=1)

    ze = rows(0)
    zo = rows(1)
    m = FOURIER_GROUP
    e = (jnp.dot(tab_bf[0], ze[:, :m], preferred_element_type=F32)
         + jnp.dot(tab_bf[1], ze[:, m:], preferred_element_type=F32))
    o = (jnp.dot(tab_bf[2], zo[:, :m], preferred_element_type=F32)
         + jnp.dot(tab_bf[3], zo[:, m:], preferred_element_type=F32))
    o_ref[0, :n2, :] = (e + o).astype(BF16)
    o_ref[0, n2:, :] = (e - o).astype(BF16)


def _fourier_call(h, w1, tab):
    b, s, d = h.shape
    blk = pl.BlockSpec((1, s, FOURIER_GROUP), lambda i, g: (i, 0, g))
    return pl.pallas_call(
        _fourier_kernel,
        grid=(b, d // FOURIER_GROUP),
        in_specs=[blk, _resident(w1.shape), _resident(tab.shape)],
        out_specs=blk,
        out_shape=jax.ShapeDtypeStruct(h.shape, BF16),
        scratch_shapes=[pltpu.VMEM(w1.shape, BF16), pltpu.VMEM(tab.shape, BF16),
                        pltpu.VMEM((2 * FOURIER_GROUP // LANES, s, LANES), F32)],
        compiler_params=_params(2),
        name="fourier",
    )(h, w1, tab)


def _dft_tables(n_pos):
    m = FOURIER_GROUP
    km = (np.outer(np.arange(m), np.arange(m)) % m) * (2.0 * np.pi / m)
    w1 = np.concatenate([np.cos(km), np.sin(km)], axis=1) / math.sqrt(m)
    k = np.arange(n_pos // 2)
    tabs = []
    for parity in range(2):
        kn = (np.outer(k, 2 * k + parity) % n_pos) * (2.0 * np.pi / n_pos)
        tabs += [np.cos(kn) / math.sqrt(n_pos), -np.sin(kn) / math.sqrt(n_pos)]
    return jnp.asarray(w1, F32), jnp.asarray(np.stack(tabs), F32)


def _rope_tables(n_pos):
    pos = np.arange(n_pos)
    inv_freq = ROPE_THETA ** (-(np.arange(ROPE_PAIRS, dtype=np.float64) / ROPE_PAIRS))
    lane = np.arange(LANES)
    axis = (lane % HEAD_DIM) // 32
    coord = np.where(axis[None, :] == 0, (pos // GRID_W)[:, None], (pos % GRID_W)[:, None])
    ang = coord * inv_freq[lane % ROPE_PAIRS][None, :]
    sign = np.where((lane % 32) < 16, -1.0, 1.0)[None, :]
    return jnp.asarray(np.cos(ang), F32), jnp.asarray(np.sin(ang) * sign, F32)


def kernel(x, c, ctx, c_ctx, norm_g, w_mod, b_mod, ffn1_w_gu, ffn1_w_d, ffn2_w_gu, ffn2_w_d,
           attn_w_qkv, attn_w_o, attn_q_g, attn_k_g, attn_lam_q1, attn_lam_k1, attn_lam_q2,
           attn_lam_k2, attn_sub_g, fourier_w):
    b, s, d = x.shape
    cond = jnp.concatenate([c, c_ctx[None, :], jnp.zeros((COND_ROWS - b - 1, d), F32)], axis=0)
    mods = _ada_call(cond, w_mod, b_mod)
    mx = [mods[i, :b].reshape(b, N_MOD, d) for i in range(2)]
    my0 = mods[0, b:b + 1].reshape(1, N_MOD, d)

    wgu1, wd1 = ffn1_w_gu.astype(BF16), ffn1_w_d.astype(BF16)
    wgu2, wd2 = ffn2_w_gu.astype(BF16), ffn2_w_d.astype(BF16)
    w_qkv = attn_w_qkv[0].astype(BF16)
    w_o = attn_w_o[0].astype(BF16)
    w_f = fourier_w[0].astype(BF16)

    x, hx = _ffn_call(x, mx[0], norm_g[0], wgu1, wd1, 0, ffn_idx=0, emit_h=True)
    _, hy = _ffn_call(ctx, my0, norm_g[0], wgu1, wd1, 0, ffn_idx=0, emit_h=True)
    cos_t, sin_t = _rope_tables(s)
    gq = jnp.tile(attn_q_g[0], 2)[None, :]
    gk = jnp.tile(attn_k_g[0], 2)[None, :]
    q_scale = math.log2(math.e) / math.sqrt(HEAD_DIM)
    q, k, v = _qkv_call(hx, w_qkv, gq, gk, cos_t, sin_t, rope=True, q_scale=q_scale)
    ky, vy = _qkv_call(hy, w_qkv, gq, gk, cos_t, sin_t, rope=False, q_scale=1.0)
    lambda_init = 0.8 - 0.6 * math.exp(-0.3 * 0)
    o = _attn_call(q, k, v, ky, vy, attn_lam_q1[0:1], attn_lam_k1[0:1], attn_lam_q2[0:1],
                   attn_lam_k2[0:1], attn_sub_g[0:1], gq, gk, lambda_init=lambda_init,
                   q_scale=q_scale)
    x = _ffn_call(x, mx[0], norm_g[0], wgu2, wd2, 0, ffn_idx=1, mix=(o, w_o))

    x, hx = _ffn_call(x, mx[1], norm_g[1], wgu1, wd1, 1, ffn_idx=0, emit_h=True)
    f = _fourier_call(hx, *_dft_tables(s))
    x = _ffn_call(x, mx[1], norm_g[1], wgu2, wd2, 1, ffn_idx=1, mix=(f, w_f))
    return x
```

```python
import functools
import math

import numpy as np
import jax
import jax.numpy as jnp
from jax import lax
from jax.experimental import pallas as pl
from jax.experimental.pallas import tpu as pltpu

F32 = jnp.float32
BF16 = jnp.bfloat16

D_MODEL = 1024
N_MOD = 9
N_HEADS = 8
HEAD_DIM = 64
V_DIM = 2 * HEAD_DIM
GRID_W = 64
ROPE_THETA = 10000.0
ROPE_PAIRS = HEAD_DIM // 4
N_FOURIER_GROUPS = 4
FOURIER_GROUP = D_MODEL // N_FOURIER_GROUPS
D_FF = 2816
EPS = 1e-6
EXP2_SAFE = 100.0

LANES = 128
MXU_DIM = 256
COND_ROWS = 24
VMEM_LIMIT = 56 * 1024 * 1024

FFN_TOKENS = 1024
FFN_SUBTILE = 512
FFN_STAGGER = 2
FFN_CHUNK = 256
QKV_TOKENS = 1024
ATTN_Q_TILE = 1024
ADA_COLS = 1024


def _resident(shape):
    nd = len(shape)
    return pl.BlockSpec(shape, lambda *_: (0,) * nd, pipeline_mode=pl.Buffered(1))


def _params(n_axes):
    return pltpu.CompilerParams(dimension_semantics=("arbitrary",) * n_axes,
                                vmem_limit_bytes=VMEM_LIMIT)


def _ada_kernel(cond_ref, w_ref, b_ref, o_ref):
    cnd = cond_ref[...]
    act = (cnd * jax.nn.sigmoid(cnd)).astype(BF16)
    o_ref[0] = jnp.dot(act, w_ref[0].astype(BF16), preferred_element_type=F32) + b_ref[0]


def _ada_call(cond, w_mod, b_mod):
    depth, _, width = w_mod.shape
    return pl.pallas_call(
        _ada_kernel,
        grid=(depth, width // ADA_COLS),
        in_specs=[
            pl.BlockSpec((COND_ROWS, D_MODEL), lambda i, j: (0, 0)),
            pl.BlockSpec((1, D_MODEL, ADA_COLS), lambda i, j: (i, 0, j)),
            pl.BlockSpec((1, 1, ADA_COLS), lambda i, j: (i, 0, j)),
        ],
        out_specs=pl.BlockSpec((1, COND_ROWS, ADA_COLS), lambda i, j: (i, 0, j)),
        out_shape=jax.ShapeDtypeStruct((depth, COND_ROWS, width), F32),
        compiler_params=_params(2),
        name="ada",
    )(cond, w_mod, b_mod.reshape(depth, 1, width))


def _rms_mod(x, gain, shift, scale):
    ms = jnp.mean(x * x, axis=-1, keepdims=True)
    return x * lax.rsqrt(ms + EPS) * (gain * (1.0 + scale)) + shift


def _ffn_kernel(*refs, premix, emit_h, ffn_idx):
    x_ref, mod_ref, gain_ref = refs[:3]
    refs = refs[3:]
    if premix:
        t_ref, wt_ref = refs[:2]
        refs = refs[2:]
    wgu_ref, wd_ref, out_ref = refs[:3]
    ks = 6 if ffn_idx else 0
    grow = 2 if ffn_idx else 0
    sub = min(FFN_SUBTILE, x_ref.shape[1])

    def chain(st):
        rows = slice(st * sub, (st + 1) * sub)
        x = x_ref[0, rows, :]
        if premix:
            x = x + mod_ref[0, 5:6, :] * jnp.dot(t_ref[0, rows, :], wt_ref[...],
                                                 preferred_element_type=F32)
        h = _rms_mod(x, gain_ref[grow:grow + 1, :], mod_ref[0, ks:ks + 1, :],
                     mod_ref[0, ks + 1:ks + 2, :]).astype(BF16)
        yield
        acc = jnp.zeros(x.shape, F32)
        for c in range(D_FF // FFN_CHUNK):
            lo = c * FFN_CHUNK
            g = jnp.dot(h, wgu_ref[:, lo:lo + FFN_CHUNK], preferred_element_type=F32)
            u = jnp.dot(h, wgu_ref[:, D_FF + lo:D_FF + lo + FFN_CHUNK], preferred_element_type=F32)
            a = (g * jax.nn.sigmoid(g) * u).astype(BF16)
            acc = acc + jnp.dot(a, wd_ref[lo:lo + FFN_CHUNK, :], preferred_element_type=F32)
            yield
        out = x + (0.5 * mod_ref[0, ks + 2:ks + 3, :]) * acc
        out_ref[0, rows, :] = out
        if emit_h:
            refs[3][0, rows, :] = _rms_mod(out, gain_ref[1:2, :], mod_ref[0, 3:4, :],
                                           mod_ref[0, 4:5, :]).astype(BF16)

    chains = [chain(st) for st in range(x_ref.shape[1] // sub)]
    started = 0
    live = []
    step = 0
    while live or started < len(chains):
        if started < len(chains) and step % FFN_STAGGER == 0:
            live.append(chains[started])
            started += 1
        for ch in list(live):
            if next(ch, StopIteration) is StopIteration:
                live.remove(ch)
        step += 1


def _resident_layer(shape, layer):
    tail = (0,) * (len(shape) - 1)
    return pl.BlockSpec((None,) + tuple(shape[1:]), lambda *_: (layer,) + tail,
                        pipeline_mode=pl.Buffered(1))


def _ffn_call(x, mod, gains, w_gu, w_d, layer, *, ffn_idx, mix=None, emit_h=False):
    b, t, d = x.shape
    tm = min(FFN_TOKENS, t)
    tok = pl.BlockSpec((1, tm, d), lambda i, j: (i, j, 0))
    mod_map = (lambda i, j: (i, 0, 0)) if mod.shape[0] == b else (lambda i, j: (0, 0, 0))
    in_specs = [tok, pl.BlockSpec((1, N_MOD, d), mod_map), _resident(gains.shape)]
    args = [x, mod, gains]
    if mix is not None:
        in_specs += [tok, _resident(mix[1].shape)]
        args += [mix[0], mix[1]]
    in_specs += [_resident_layer(w_gu.shape, layer), _resident_layer(w_d.shape, layer)]
    args += [w_gu, w_d]
    out_shape = [jax.ShapeDtypeStruct(x.shape, F32)]
    out_specs = [tok]
    if emit_h:
        out_shape.append(jax.ShapeDtypeStruct(x.shape, BF16))
        out_specs.append(tok)
    res = pl.pallas_call(
        functools.partial(_ffn_kernel, premix=mix is not None, emit_h=emit_h, ffn_idx=ffn_idx),
        grid=(b, t // tm),
        in_specs=in_specs,
        out_specs=out_specs,
        out_shape=out_shape,
        compiler_params=_params(2),
        name="ffn",
    )(*args)
    return res if emit_h else res[0]


def _group_ones():
    r = lax.broadcasted_iota(jnp.int32, (MXU_DIM, MXU_DIM), 0) // HEAD_DIM
    c = lax.broadcasted_iota(jnp.int32, (MXU_DIM, MXU_DIM), 1) // HEAD_DIM
    return (r == c).astype(BF16)


def _qk_norm_rope(t, gain, cos_g, sin_g, first_half, ones_bd, out_ref):
    sq = (t * t).astype(BF16)
    for j in range(t.shape[1] // MXU_DIM):
        ssq = jnp.dot(sq[:, j * MXU_DIM:(j + 1) * MXU_DIM], ones_bd, preferred_element_type=F32)
        inv = lax.rsqrt(ssq * (1.0 / HEAD_DIM) + EPS)
        for jj in range(MXU_DIM // LANES):
            lo = j * MXU_DIM + jj * LANES
            tb = t[:, lo:lo + LANES]
            ib = inv[:, jj * LANES:(jj + 1) * LANES]
            if cos_g is None:
                ob = tb * ib * gain
            else:
                partner = jnp.where(first_half, pltpu.roll(tb, LANES - 16, 1), pltpu.roll(tb, 16, 1))
                ob = ib * (tb * cos_g + partner * sin_g)
            out_ref[0, :, lo:lo + LANES] = ob.astype(BF16)


def _qkv_kernel(h_ref, w_ref, gq_ref, gk_ref, cos_ref, sin_ref, *out_refs, rope, q_scale):
    h = h_ref[0]
    ones_bd = _group_ones()
    first_half = None
    if rope:
        lane = lax.broadcasted_iota(jnp.int32, (h.shape[0], LANES), 1)
        first_half = (lane % 32) < 16
        cos_t = cos_ref[...]
        sin_t = sin_ref[...]

        def partner_gain(g_ref):
            g8 = jnp.broadcast_to(g_ref[...], (8, LANES))
            fh = (lax.broadcasted_iota(jnp.int32, (8, LANES), 1) % 32) < 16
            return jnp.where(fh, pltpu.roll(g8, LANES - 16, 1), pltpu.roll(g8, 16, 1))[:1]

        q_ref, k_ref, v_ref = out_refs
        q = jnp.dot(h, w_ref[:, :D_MODEL], preferred_element_type=F32)
        _qk_norm_rope(q, None, cos_t * (gq_ref[...] * q_scale), sin_t * (partner_gain(gq_ref) * q_scale),
                      first_half, ones_bd, q_ref)
        k = jnp.dot(h, w_ref[:, D_MODEL:2 * D_MODEL], preferred_element_type=F32)
        _qk_norm_rope(k, None, cos_t * gk_ref[...], sin_t * partner_gain(gk_ref),
                      first_half, ones_bd, k_ref)
    else:
        k_ref, v_ref = out_refs
        k = jnp.dot(h, w_ref[:, D_MODEL:2 * D_MODEL], preferred_element_type=F32)
        _qk_norm_rope(k, gk_ref[...], None, None, None, ones_bd, k_ref)
    v_ref[0] = jnp.dot(h, w_ref[:, 2 * D_MODEL:], preferred_element_type=F32).astype(BF16)


def _qkv_call(h, w_qkv, gq, gk, cos_t, sin_t, *, rope, q_scale):
    b, t, d = h.shape
    tm = min(QKV_TOKENS, t)
    tok = pl.BlockSpec((1, tm, d), lambda i, j: (i, j, 0))
    tab = pl.BlockSpec((tm, LANES), lambda i, j: (j, 0))
    n_out = 3 if rope else 2
    return pl.pallas_call(
        functools.partial(_qkv_kernel, rope=rope, q_scale=q_scale),
        grid=(b, t // tm),
        in_specs=[tok, _resident(w_qkv.shape), _resident(gq.shape), _resident(gk.shape), tab, tab],
        out_specs=[tok] * n_out,
        out_shape=[jax.ShapeDtypeStruct((b, t, d), BF16)] * n_out,
        compiler_params=_params(2),
        name="qkv",
    )(h, w_qkv, gq, gk, cos_t, sin_t)


def _attn_kernel(q_ref, kl_ref, kc_ref, vl_ref, vc_ref, lq1_ref, lk1_ref, lq2_ref, lk2_ref,
                 subg_ref, gq_ref, gk_ref, o_ref, kcat, vt, *, lambda_init, n_ctx, q_scale):
    @pl.when(pl.program_id(2) == 0)
    def _():
        kcat[0:n_ctx, :] = kc_ref[0]
        kcat[n_ctx:, :] = kl_ref[0]
        vt[:, 0:n_ctx] = vc_ref[0].T
        vt[:, n_ctx:] = vl_ref[0].T

    bound = (HEAD_DIM * q_scale * 1.05) * jnp.max(jnp.abs(gq_ref[...])) * jnp.max(jnp.abs(gk_ref[...]))
    shift_free = bound < EXP2_SAFE

    def attend(subtract_max):
        lam = (jnp.exp(jnp.sum(lq1_ref[...] * lk1_ref[...], axis=-1, keepdims=True))
               - jnp.exp(jnp.sum(lq2_ref[...] * lk2_ref[...], axis=-1, keepdims=True))
               + lambda_init)
        q = q_ref[0]
        lane = lax.broadcasted_iota(jnp.int32, q.shape, 1)
        outs = []
        for comp in range(2):
            sel = (lane < HEAD_DIM) if comp == 0 else (lane >= HEAD_DIM)
            qm = jnp.where(sel, q, jnp.zeros_like(q))
            s = lax.dot_general(kcat[...], qm, (((1,), (1,)), ((), ())),
                                preferred_element_type=F32)
            if subtract_max:
                s = s - jnp.max(s, axis=0, keepdims=True)
            e = jnp.exp2(s)
            r = jnp.dot(vt[...], e.astype(BF16), preferred_element_type=F32)
            outs.append(r / jnp.sum(e, axis=0, keepdims=True))
        o = outs[0] - lam * outs[1]
        ms = jnp.mean(o * o, axis=0, keepdims=True)
        o = o * lax.rsqrt(ms + EPS) * (subg_ref[...] * (1.0 - lambda_init))
        o_ref[0] = o.T.astype(BF16)

    pl.when(shift_free)(functools.partial(attend, False))
    pl.when(jnp.logical_not(shift_free))(functools.partial(attend, True))


def _attn_call(q, k_lat, v_lat, k_ctx, v_ctx, lq1, lk1, lq2, lk2, sub_g, gq, gk, *,
               lambda_init, q_scale):
    b, s, _ = q.shape
    n_ctx = k_ctx.shape[1]
    tq = ATTN_Q_TILE
    qspec = pl.BlockSpec((1, tq, V_DIM), lambda i, h, j: (i, j, h))
    lat = pl.BlockSpec((1, s, V_DIM), lambda i, h, j: (i, 0, h))
    ctx = pl.BlockSpec((1, n_ctx, V_DIM), lambda i, h, j: (i, 0, h))
    vec = pl.BlockSpec((1, HEAD_DIM), lambda i, h, j: (0, 0))
    row = pl.BlockSpec((1, V_DIM), lambda i, h, j: (0, 0))
    return pl.pallas_call(
        functools.partial(_attn_kernel, lambda_init=lambda_init, n_ctx=n_ctx, q_scale=q_scale),
        grid=(b, N_HEADS, s // tq),
        in_specs=[qspec, lat, ctx, lat, ctx, vec, vec, vec, vec,
                  pl.BlockSpec((V_DIM, 1), lambda i, h, j: (0, 0)), row, row],
        out_specs=qspec,
        out_shape=jax.ShapeDtypeStruct(q.shape, BF16),
        scratch_shapes=[pltpu.VMEM((n_ctx + s, V_DIM), BF16),
                        pltpu.VMEM((V_DIM, n_ctx + s), BF16)],
        compiler_params=_params(3),
        name="attn",
    )(q, k_lat, k_ctx, v_lat, v_ctx, lq1, lk1, lq2, lk2, sub_g.reshape(V_DIM, 1), gq, gk)


def _fourier_kernel(h_ref, w1_ref, tab_ref, o_ref, w1_bf, tab_bf, z_s):
    @pl.when((pl.program_id(0) == 0) & (pl.program_id(1) == 0))
    def _():
        w1_bf[...] = w1_ref[...].astype(BF16)
        for i in range(tab_ref.shape[0]):
            tab_bf[i] = tab_ref[i].astype(BF16)

    z = jnp.dot(h_ref[0], w1_bf[...], preferred_element_type=F32)
    for c in range(z_s.shape[0]):
        z_s[c] = z[:, c * LANES:(c + 1) * LANES]
    n2 = z_s.shape[1] // 2

    def rows(parity):
        return jnp.concatenate([z_s[c, pl.ds(parity, n2, stride=2), :].astype(BF16)
                                for c in range(z_s.shape[0])], axis=1)

    ze = rows(0)
    zo = rows(1)
    m = FOURIER_GROUP
    e = (jnp.dot(tab_bf[0], ze[:, :m], preferred_element_type=F32)
         + jnp.dot(tab_bf[1], ze[:, m:], preferred_element_type=F32))
    o = (jnp.dot(tab_bf[2], zo[:, :m], preferred_element_type=F32)
         + jnp.dot(tab_bf[3], zo[:, m:], preferred_element_type=F32))
    o_ref[0, :n2, :] = (e + o).astype(BF16)
    o_ref[0, n2:, :] = (e - o).astype(BF16)


def _fourier_call(h, w1, tab):
    b, s, d = h.shape
    blk = pl.BlockSpec((1, s, FOURIER_GROUP), lambda i, g: (i, 0, g))
    return pl.pallas_call(
        _fourier_kernel,
        grid=(b, d // FOURIER_GROUP),
        in_specs=[blk, _resident(w1.shape), _resident(tab.shape)],
        out_specs=blk,
        out_shape=jax.ShapeDtypeStruct(h.shape, BF16),
        scratch_shapes=[pltpu.VMEM(w1.shape, BF16), pltpu.VMEM(tab.shape, BF16),
                        pltpu.VMEM((2 * FOURIER_GROUP // LANES, s, LANES), F32)],
        compiler_params=_params(2),
        name="fourier",
    )(h, w1, tab)


def _dft_tables(n_pos):
    m = FOURIER_GROUP
    km = (np.outer(np.arange(m), np.arange(m)) % m) * (2.0 * np.pi / m)
    w1 = np.concatenate([np.cos(km), np.sin(km)], axis=1) / math.sqrt(m)
    k = np.arange(n_pos // 2)
    tabs = []
    for parity in range(2):
        kn = (np.outer(k, 2 * k + parity) % n_pos) * (2.0 * np.pi / n_pos)
        tabs += [np.cos(kn) / math.sqrt(n_pos), -np.sin(kn) / math.sqrt(n_pos)]
    return jnp.asarray(w1, F32), jnp.asarray(np.stack(tabs), F32)


def _rope_tables(n_pos):
    pos = np.arange(n_pos)
    inv_freq = ROPE_THETA ** (-(np.arange(ROPE_PAIRS, dtype=np.float64) / ROPE_PAIRS))
    lane = np.arange(LANES)
    axis = (lane % HEAD_DIM) // 32
    coord = np.where(axis[None, :] == 0, (pos // GRID_W)[:, None], (pos % GRID_W)[:, None])
    ang = coord * inv_freq[lane % ROPE_PAIRS][None, :]
    sign = np.where((lane % 32) < 16, -1.0, 1.0)[None, :]
    return jnp.asarray(np.cos(ang), F32), jnp.asarray(np.sin(ang) * sign, F32)


def kernel(x, c, ctx, c_ctx, norm_g, w_mod, b_mod, ffn1_w_gu, ffn1_w_d, ffn2_w_gu, ffn2_w_d,
           attn_w_qkv, attn_w_o, attn_q_g, attn_k_g, attn_lam_q1, attn_lam_k1, attn_lam_q2,
           attn_lam_k2, attn_sub_g, fourier_w):
    b, s, d = x.shape
    cond = jnp.concatenate([c, c_ctx[None, :], jnp.zeros((COND_ROWS - b - 1, d), F32)], axis=0)
    mods = _ada_call(cond, w_mod, b_mod)
    mx = [mods[i, :b].reshape(b, N_MOD, d) for i in range(2)]
    my0 = mods[0, b:b + 1].reshape(1, N_MOD, d)

    wgu1, wd1 = ffn1_w_gu.astype(BF16), ffn1_w_d.astype(BF16)
    wgu2, wd2 = ffn2_w_gu.astype(BF16), ffn2_w_d.astype(BF16)
    w_qkv = attn_w_qkv[0].astype(BF16)
    w_o = attn_w_o[0].astype(BF16)
    w_f = fourier_w[0].astype(BF16)

    x, hx = _ffn_call(x, mx[0], norm_g[0], wgu1, wd1, 0, ffn_idx=0, emit_h=True)
    n_ctx = ctx.shape[1]
    ctx_rows = ctx.reshape(b * n_ctx // FFN_TOKENS, FFN_TOKENS, d)
    _, hy = _ffn_call(ctx_rows, my0, norm_g[0], wgu1, wd1, 0, ffn_idx=0, emit_h=True)
    cos_t, sin_t = _rope_tables(s)
    gq = jnp.tile(attn_q_g[0], 2)[None, :]
    gk = jnp.tile(attn_k_g[0], 2)[None, :]
    q_scale = math.log2(math.e) / math.sqrt(HEAD_DIM)
    q, k, v = _qkv_call(hx, w_qkv, gq, gk, cos_t, sin_t, rope=True, q_scale=q_scale)
    ky, vy = _qkv_call(hy, w_qkv, gq, gk, cos_t, sin_t, rope=False, q_scale=1.0)
    ky, vy = ky.reshape(b, n_ctx, d), vy.reshape(b, n_ctx, d)
    lambda_init = 0.8 - 0.6 * math.exp(-0.3 * 0)
    o = _attn_call(q, k, v, ky, vy, attn_lam_q1[0:1], attn_lam_k1[0:1], attn_lam_q2[0:1],
                   attn_lam_k2[0:1], attn_sub_g[0:1], gq, gk, lambda_init=lambda_init,
                   q_scale=q_scale)
    x = _ffn_call(x, mx[0], norm_g[0], wgu2, wd2, 0, ffn_idx=1, mix=(o, w_o))

    x, hx = _ffn_call(x, mx[1], norm_g[1], wgu1, wd1, 1, ffn_idx=0, emit_h=True)
    f = _fourier_call(hx, *_dft_tables(s))
    x = _ffn_call(x, mx[1], norm_g[1], wgu2, wd2, 1, ffn_idx=1, mix=(f, w_f))
    return x
```

```python
import functools
import math

import numpy as np
import jax
import jax.numpy as jnp
from jax import lax
from jax.experimental import pallas as pl
from jax.experimental.pallas import tpu as pltpu

F32 = jnp.float32
BF16 = jnp.bfloat16

D_MODEL = 1024
N_MOD = 9
N_HEADS = 8
HEAD_DIM = 64
V_DIM = 2 * HEAD_DIM
GRID_W = 64
ROPE_THETA = 10000.0
ROPE_PAIRS = HEAD_DIM // 4
N_FOURIER_GROUPS = 4
FOURIER_GROUP = D_MODEL // N_FOURIER_GROUPS
D_FF = 2816
EPS = 1e-6
EXP2_SAFE = 100.0

LANES = 128
MXU_DIM = 256
COND_ROWS = 24
VMEM_LIMIT = 56 * 1024 * 1024

FFN_TOKENS = 1024
FFN_SUBTILE = 512
FFN_STAGGER = 2
FFN_CHUNK = 256
QKV_TOKENS = 1024
ATTN_Q_TILE = 1024
ADA_COLS = 1024


def _resident(shape):
    nd = len(shape)
    return pl.BlockSpec(shape, lambda *_: (0,) * nd, pipeline_mode=pl.Buffered(1))


def _params(n_axes):
    return pltpu.CompilerParams(dimension_semantics=("arbitrary",) * n_axes,
                                vmem_limit_bytes=VMEM_LIMIT)


def _ada_kernel(cond_ref, w_ref, b_ref, o_ref):
    cnd = cond_ref[...]
    act = (cnd * jax.nn.sigmoid(cnd)).astype(BF16)
    o_ref[0] = jnp.dot(act, w_ref[0].astype(BF16), preferred_element_type=F32) + b_ref[0]


def _ada_call(cond, w_mod, b_mod):
    depth, _, width = w_mod.shape
    return pl.pallas_call(
        _ada_kernel,
        grid=(depth, width // ADA_COLS),
        in_specs=[
            pl.BlockSpec((COND_ROWS, D_MODEL), lambda i, j: (0, 0)),
            pl.BlockSpec((1, D_MODEL, ADA_COLS), lambda i, j: (i, 0, j)),
            pl.BlockSpec((1, 1, ADA_COLS), lambda i, j: (i, 0, j)),
        ],
        out_specs=pl.BlockSpec((1, COND_ROWS, ADA_COLS), lambda i, j: (i, 0, j)),
        out_shape=jax.ShapeDtypeStruct((depth, COND_ROWS, width), F32),
        compiler_params=_params(2),
        name="ada",
    )(cond, w_mod, b_mod.reshape(depth, 1, width))


def _rms_mod(x, gain, shift, scale):
    ms = jnp.mean(x * x, axis=-1, keepdims=True)
    return x * lax.rsqrt(ms + EPS) * (gain * (1.0 + scale)) + shift


def _ffn_kernel(*refs, premix, emit_h, ffn_idx):
    x_ref, mod_ref, gain_ref = refs[:3]
    refs = refs[3:]
    if premix:
        t_ref, wt_ref = refs[:2]
        refs = refs[2:]
    wgu_ref, wd_ref, out_ref = refs[:3]
    ks = 6 if ffn_idx else 0
    grow = 2 if ffn_idx else 0
    sub = min(FFN_SUBTILE, x_ref.shape[1])

    def chain(st):
        rows = slice(st * sub, (st + 1) * sub)
        x = x_ref[0, rows, :]
        if premix:
            x = x + mod_ref[0, 5:6, :] * jnp.dot(t_ref[0, rows, :], wt_ref[...],
                                                 preferred_element_type=F32)
        h = _rms_mod(x, gain_ref[grow:grow + 1, :], mod_ref[0, ks:ks + 1, :],
                     mod_ref[0, ks + 1:ks + 2, :]).astype(BF16)
        yield
        acc = jnp.zeros(x.shape, F32)
        for c in range(D_FF // FFN_CHUNK):
            lo = c * FFN_CHUNK
            g = jnp.dot(h, wgu_ref[:, lo:lo + FFN_CHUNK], preferred_element_type=F32)
            u = jnp.dot(h, wgu_ref[:, D_FF + lo:D_FF + lo + FFN_CHUNK], preferred_element_type=F32)
            a = (g * jax.nn.sigmoid(g) * u).astype(BF16)
            acc = acc + jnp.dot(a, wd_ref[lo:lo + FFN_CHUNK, :], preferred_element_type=F32)
            yield
        out = x + (0.5 * mod_ref[0, ks + 2:ks + 3, :]) * acc
        out_ref[0, rows, :] = out
        if emit_h:
            refs[3][0, rows, :] = _rms_mod(out, gain_ref[1:2, :], mod_ref[0, 3:4, :],
                                           mod_ref[0, 4:5, :]).astype(BF16)

    chains = [chain(st) for st in range(x_ref.shape[1] // sub)]
    started = 0
    live = []
    step = 0
    while live or started < len(chains):
        if started < len(chains) and step % FFN_STAGGER == 0:
            live.append(chains[started])
            started += 1
        for ch in list(live):
            if next(ch, StopIteration) is StopIteration:
                live.remove(ch)
        step += 1


def _resident_layer(shape, layer):
    tail = (0,) * (len(shape) - 1)
    return pl.BlockSpec((None,) + tuple(shape[1:]), lambda *_: (layer,) + tail,
                        pipeline_mode=pl.Buffered(1))


def _ffn_call(x, mod, gains, w_gu, w_d, layer, *, ffn_idx, mix=None, emit_h=False):
    b, t, d = x.shape
    tm = min(FFN_TOKENS, t)
    tok = pl.BlockSpec((1, tm, d), lambda i, j: (i, j, 0))
    mod_map = (lambda i, j: (i, 0, 0)) if mod.shape[0] == b else (lambda i, j: (0, 0, 0))
    in_specs = [tok, pl.BlockSpec((1, N_MOD, d), mod_map), _resident(gains.shape)]
    args = [x, mod, gains]
    if mix is not None:
        in_specs += [tok, _resident(mix[1].shape)]
        args += [mix[0], mix[1]]
    in_specs += [_resident_layer(w_gu.shape, layer), _resident_layer(w_d.shape, layer)]
    args += [w_gu, w_d]
    out_shape = [jax.ShapeDtypeStruct(x.shape, F32)]
    out_specs = [tok]
    if emit_h:
        out_shape.append(jax.ShapeDtypeStruct(x.shape, BF16))
        out_specs.append(tok)
    res = pl.pallas_call(
        functools.partial(_ffn_kernel, premix=mix is not None, emit_h=emit_h, ffn_idx=ffn_idx),
        grid=(b, t // tm),
        in_specs=in_specs,
        out_specs=out_specs,
        out_shape=out_shape,
        compiler_params=_params(2),
        name="ffn",
    )(*args)
    return res if emit_h else res[0]


def _swap_rotary_halves(t):
    return jnp.concatenate([t[..., 16:32, :], t[..., 0:16, :], t[..., 48:64, :], t[..., 32:48, :]],
                           axis=-2)


def _qkv_kernel(h_ref, wt_ref, gq_ref, gk_ref, cos_ref, sin_ref, *out_refs, rope, q_scale):
    h = h_ref[0]
    tm = h.shape[0]
    groups = D_MODEL // HEAD_DIM

    def project(lo):
        return lax.dot_general(wt_ref[lo:lo + D_MODEL, :], h, (((1,), (1,)), ((), ())),
                               preferred_element_type=F32)

    def normed(t, g_ref, scale):
        t = t.reshape(groups, HEAD_DIM, tm)
        inv = lax.rsqrt(jnp.sum(t * t, axis=1, keepdims=True) * (1.0 / HEAD_DIM) + EPS)
        gain = pltpu.repeat(g_ref[...] * scale, tm // LANES, axis=1)
        if not rope:
            return (t * inv * gain).reshape(D_MODEL, tm)
        cos_g = cos_ref[...] * gain
        sin_g = sin_ref[...] * _swap_rotary_halves(gain)
        return (inv * (t * cos_g + _swap_rotary_halves(t) * sin_g)).reshape(D_MODEL, tm)

    if rope:
        qt_ref, k_ref, vt_ref = out_refs
        qt_ref[0] = normed(project(0), gq_ref, q_scale).astype(BF16)
    else:
        k_ref, vt_ref = out_refs
    k_ref[0] = normed(project(D_MODEL), gk_ref, 1.0).T.astype(BF16)
    vt_ref[0] = project(2 * D_MODEL).astype(BF16)


def _qkv_call(h, wt_qkv, gq, gk, cos_t, sin_t, *, rope, q_scale):
    b, t, d = h.shape
    tm = min(QKV_TOKENS, t)
    tok = pl.BlockSpec((1, tm, d), lambda i, j: (i, j, 0))
    tok_t = pl.BlockSpec((1, d, tm), lambda i, j: (i, 0, j))
    tab = pl.BlockSpec((HEAD_DIM, tm), lambda i, j: (0, j))
    shape_t = jax.ShapeDtypeStruct((b, d, t), BF16)
    return pl.pallas_call(
        functools.partial(_qkv_kernel, rope=rope, q_scale=q_scale),
        grid=(b, t // tm),
        in_specs=[tok, _resident(wt_qkv.shape), _resident(gq.shape), _resident(gk.shape), tab, tab],
        out_specs=([tok_t] if rope else []) + [tok, tok_t],
        out_shape=([shape_t] if rope else []) + [jax.ShapeDtypeStruct((b, t, d), BF16), shape_t],
        compiler_params=_params(2),
        name="qkv",
    )(h, wt_qkv, gq, gk, cos_t, sin_t)


def _attn_kernel(q_ref, kl_ref, kc_ref, vl_ref, vc_ref, lq1_ref, lk1_ref, lq2_ref, lk2_ref,
                 subg_ref, gq_ref, gk_ref, o_ref, kcat, vt, *, lambda_init, n_ctx, q_scale):
    @pl.when(pl.program_id(2) == 0)
    def _():
        kcat[0:n_ctx, :] = kc_ref[0]
        kcat[n_ctx:, :] = kl_ref[0]
        vt[:, 0:n_ctx] = vc_ref[0]
        vt[:, n_ctx:] = vl_ref[0]

    bound = (HEAD_DIM * q_scale * 1.05) * jnp.max(jnp.abs(gq_ref[...])) * jnp.max(jnp.abs(gk_ref[...]))
    shift_free = bound < EXP2_SAFE

    def attend(subtract_max):
        lam = (jnp.exp(jnp.sum(lq1_ref[...] * lk1_ref[...], axis=-1, keepdims=True))
               - jnp.exp(jnp.sum(lq2_ref[...] * lk2_ref[...], axis=-1, keepdims=True))
               + lambda_init)
        qt = q_ref[0]
        dim = lax.broadcasted_iota(jnp.int32, qt.shape, 0)
        outs = []
        for comp in range(2):
            sel = (dim < HEAD_DIM) if comp == 0 else (dim >= HEAD_DIM)
            qm = jnp.where(sel, qt, jnp.zeros_like(qt))
            s = jnp.dot(kcat[...], qm, preferred_element_type=F32)
            if subtract_max:
                s = s - jnp.max(s, axis=0, keepdims=True)
            e = jnp.exp2(s)
            r = jnp.dot(vt[...], e.astype(BF16), preferred_element_type=F32)
            outs.append(r / jnp.sum(e, axis=0, keepdims=True))
        o = outs[0] - lam * outs[1]
        ms = jnp.mean(o * o, axis=0, keepdims=True)
        o = o * lax.rsqrt(ms + EPS) * (subg_ref[...] * (1.0 - lambda_init))
        o_ref[0] = o.T.astype(BF16)

    pl.when(shift_free)(functools.partial(attend, False))
    pl.when(jnp.logical_not(shift_free))(functools.partial(attend, True))


def _attn_call(qt, k_lat, vt_lat, k_ctx, vt_ctx, lq1, lk1, lq2, lk2, sub_g, gq, gk, *,
               lambda_init, q_scale):
    b, d, s = qt.shape
    n_ctx = k_ctx.shape[1]
    tq = ATTN_Q_TILE
    vec = pl.BlockSpec((1, HEAD_DIM), lambda i, h, j: (0, 0))
    gain = pl.BlockSpec(gq.shape, lambda i, h, j: (0, 0))
    return pl.pallas_call(
        functools.partial(_attn_kernel, lambda_init=lambda_init, n_ctx=n_ctx, q_scale=q_scale),
        grid=(b, N_HEADS, s // tq),
        in_specs=[pl.BlockSpec((1, V_DIM, tq), lambda i, h, j: (i, h, j)),
                  pl.BlockSpec((1, s, V_DIM), lambda i, h, j: (i, 0, h)),
                  pl.BlockSpec((1, n_ctx, V_DIM), lambda i, h, j: (i, 0, h)),
                  pl.BlockSpec((1, V_DIM, s), lambda i, h, j: (i, h, 0)),
                  pl.BlockSpec((1, V_DIM, n_ctx), lambda i, h, j: (i, h, 0)),
                  vec, vec, vec, vec, pl.BlockSpec((V_DIM, 1), lambda i, h, j: (0, 0)), gain, gain],
        out_specs=pl.BlockSpec((1, tq, V_DIM), lambda i, h, j: (i, j, h)),
        out_shape=jax.ShapeDtypeStruct((b, s, d), BF16),
        scratch_shapes=[pltpu.VMEM((n_ctx + s, V_DIM), BF16),
                        pltpu.VMEM((V_DIM, n_ctx + s), BF16)],
        compiler_params=_params(3),
        name="attn",
    )(qt, k_lat, k_ctx, vt_lat, vt_ctx, lq1, lk1, lq2, lk2, sub_g.reshape(V_DIM, 1), gq, gk)


def _fourier_kernel(h_ref, w1_ref, tab_ref, o_ref, w1_bf, tab_bf, z_s):
    @pl.when((pl.program_id(0) == 0) & (pl.program_id(1) == 0))
    def _():
        w1_bf[...] = w1_ref[...].astype(BF16)
        for i in range(tab_ref.shape[0]):
            tab_bf[i] = tab_ref[i].astype(BF16)

    z = jnp.dot(h_ref[0], w1_bf[...], preferred_element_type=F32)
    for c in range(z_s.shape[0]):
        z_s[c] = z[:, c * LANES:(c + 1) * LANES]
    n2 = z_s.shape[1] // 2

    def rows(parity):
        return jnp.concatenate([z_s[c, pl.ds(parity, n2, stride=2), :].astype(BF16)
                                for c in range(z_s.shape[0])], axis=1)

    ze = rows(0)
    zo = rows(1)
    m = FOURIER_GROUP
    e = (jnp.dot(tab_bf[0], ze[:, :m], preferred_element_type=F32)
         + jnp.dot(tab_bf[1], ze[:, m:], preferred_element_type=F32))
    o = (jnp.dot(tab_bf[2], zo[:, :m], preferred_element_type=F32)
         + jnp.dot(tab_bf[3], zo[:, m:], preferred_element_type=F32))
    o_ref[0, :n2, :] = (e + o).astype(BF16)
    o_ref[0, n2:, :] = (e - o).astype(BF16)


def _fourier_call(h, w1, tab):
    b, s, d = h.shape
    blk = pl.BlockSpec((1, s, FOURIER_GROUP), lambda i, g: (i, 0, g))
    return pl.pallas_call(
        _fourier_kernel,
        grid=(b, d // FOURIER_GROUP),
        in_specs=[blk, _resident(w1.shape), _resident(tab.shape)],
        out_specs=blk,
        out_shape=jax.ShapeDtypeStruct(h.shape, BF16),
        scratch_shapes=[pltpu.VMEM(w1.shape, BF16), pltpu.VMEM(tab.shape, BF16),
                        pltpu.VMEM((2 * FOURIER_GROUP // LANES, s, LANES), F32)],
        compiler_params=_params(2),
        name="fourier",
    )(h, w1, tab)


def _dft_tables(n_pos):
    m = FOURIER_GROUP
    km = (np.outer(np.arange(m), np.arange(m)) % m) * (2.0 * np.pi / m)
    w1 = np.concatenate([np.cos(km), np.sin(km)], axis=1) / math.sqrt(m)
    k = np.arange(n_pos // 2)
    tabs = []
    for parity in range(2):
        kn = (np.outer(k, 2 * k + parity) % n_pos) * (2.0 * np.pi / n_pos)
        tabs += [np.cos(kn) / math.sqrt(n_pos), -np.sin(kn) / math.sqrt(n_pos)]
    return jnp.asarray(w1, F32), jnp.asarray(np.stack(tabs), F32)


def _rope_tables(n_pos):
    pos = np.arange(n_pos)
    inv_freq = ROPE_THETA ** (-(np.arange(ROPE_PAIRS, dtype=np.float64) / ROPE_PAIRS))
    dim = np.arange(HEAD_DIM)
    coord = np.where((dim // 32)[:, None] == 0, (pos // GRID_W)[None, :], (pos % GRID_W)[None, :])
    ang = coord * inv_freq[dim % ROPE_PAIRS][:, None]
    sign = np.where((dim % 32) < 16, -1.0, 1.0)[:, None]
    return jnp.asarray(np.cos(ang), F32), jnp.asarray(np.sin(ang) * sign, F32)


def kernel(x, c, ctx, c_ctx, norm_g, w_mod, b_mod, ffn1_w_gu, ffn1_w_d, ffn2_w_gu, ffn2_w_d,
           attn_w_qkv, attn_w_o, attn_q_g, attn_k_g, attn_lam_q1, attn_lam_k1, attn_lam_q2,
           attn_lam_k2, attn_sub_g, fourier_w):
    b, s, d = x.shape
    cond = jnp.concatenate([c, c_ctx[None, :], jnp.zeros((COND_ROWS - b - 1, d), F32)], axis=0)
    mods = _ada_call(cond, w_mod, b_mod)
    mx = [mods[i, :b].reshape(b, N_MOD, d) for i in range(2)]
    my0 = mods[0, b:b + 1].reshape(1, N_MOD, d)

    wgu1, wd1 = ffn1_w_gu.astype(BF16), ffn1_w_d.astype(BF16)
    wgu2, wd2 = ffn2_w_gu.astype(BF16), ffn2_w_d.astype(BF16)
    wt_qkv = attn_w_qkv[0].T.astype(BF16)
    w_o = attn_w_o[0].astype(BF16)
    w_f = fourier_w[0].astype(BF16)

    x, hx = _ffn_call(x, mx[0], norm_g[0], wgu1, wd1, 0, ffn_idx=0, emit_h=True)
    n_ctx = ctx.shape[1]
    ctx_rows = ctx.reshape(b * n_ctx // FFN_TOKENS, FFN_TOKENS, d)
    _, hy = _ffn_call(ctx_rows, my0, norm_g[0], wgu1, wd1, 0, ffn_idx=0, emit_h=True)
    cos_t, sin_t = _rope_tables(s)
    gq = jnp.broadcast_to(attn_q_g[0][:, None], (HEAD_DIM, LANES))
    gk = jnp.broadcast_to(attn_k_g[0][:, None], (HEAD_DIM, LANES))
    q_scale = math.log2(math.e) / math.sqrt(HEAD_DIM)
    qt, k, vt = _qkv_call(hx, wt_qkv, gq, gk, cos_t, sin_t, rope=True, q_scale=q_scale)
    ky, vyt = _qkv_call(hy.reshape(b, n_ctx, d), wt_qkv, gq, gk, cos_t, sin_t, rope=False,
                        q_scale=1.0)
    lambda_init = 0.8 - 0.6 * math.exp(-0.3 * 0)
    o = _attn_call(qt, k, vt, ky, vyt, attn_lam_q1[0:1], attn_lam_k1[0:1], attn_lam_q2[0:1],
                   attn_lam_k2[0:1], attn_sub_g[0:1], gq, gk, lambda_init=lambda_init,
                   q_scale=q_scale)
    x = _ffn_call(x, mx[0], norm_g[0], wgu2, wd2, 0, ffn_idx=1, mix=(o, w_o))

    x, hx = _ffn_call(x, mx[1], norm_g[1], wgu1, wd1, 1, ffn_idx=0, emit_h=True)
    f = _fourier_call(hx, *_dft_tables(s))
    x = _ffn_call(x, mx[1], norm_g[1], wgu2, wd2, 1, ffn_idx=1, mix=(f, w_f))
    return x
```

```python
import functools
import math

import numpy as np
import jax
import jax.numpy as jnp
from jax import lax
from jax.experimental import pallas as pl
from jax.experimental.pallas import tpu as pltpu

F32 = jnp.float32
BF16 = jnp.bfloat16

D_MODEL = 1024
N_MOD = 9
N_HEADS = 8
HEAD_DIM = 64
V_DIM = 2 * HEAD_DIM
GRID_W = 64
ROPE_THETA = 10000.0
ROPE_PAIRS = HEAD_DIM // 4
N_FOURIER_GROUPS = 4
FOURIER_GROUP = D_MODEL // N_FOURIER_GROUPS
D_FF = 2816
EPS = 1e-6
EXP2_SAFE = 100.0

LANES = 128
MXU_DIM = 256
COND_ROWS = 24
VMEM_LIMIT = 56 * 1024 * 1024

FFN_TOKENS = 1024
FFN_SUBTILE = 512
FFN_STAGGER = 2
FFN_CHUNK = 256
QKV_TOKENS = 1024
ATTN_Q_TILE = 1024
ATTN_SUBTILE = 1024
ADA_COLS = 1024


def _resident(shape):
    nd = len(shape)
    return pl.BlockSpec(shape, lambda *_: (0,) * nd, pipeline_mode=pl.Buffered(1))


def _params(n_axes):
    return pltpu.CompilerParams(dimension_semantics=("arbitrary",) * n_axes,
                                vmem_limit_bytes=VMEM_LIMIT)


def _trace_staggered(chains, stagger):
    started, live, step = 0, [], 0
    while live or started < len(chains):
        if started < len(chains) and step % stagger == 0:
            live.append(chains[started])
            started += 1
        for ch in list(live):
            if next(ch, StopIteration) is StopIteration:
                live.remove(ch)
        step += 1


def _ada_kernel(cond_ref, w_ref, b_ref, o_ref):
    cnd = cond_ref[...]
    act = (cnd * jax.nn.sigmoid(cnd)).astype(BF16)
    o_ref[0] = jnp.dot(act, w_ref[0].astype(BF16), preferred_element_type=F32) + b_ref[0]


def _ada_call(cond, w_mod, b_mod):
    depth, _, width = w_mod.shape
    return pl.pallas_call(
        _ada_kernel,
        grid=(depth, width // ADA_COLS),
        in_specs=[
            pl.BlockSpec((COND_ROWS, D_MODEL), lambda i, j: (0, 0)),
            pl.BlockSpec((1, D_MODEL, ADA_COLS), lambda i, j: (i, 0, j)),
            pl.BlockSpec((1, 1, ADA_COLS), lambda i, j: (i, 0, j)),
        ],
        out_specs=pl.BlockSpec((1, COND_ROWS, ADA_COLS), lambda i, j: (i, 0, j)),
        out_shape=jax.ShapeDtypeStruct((depth, COND_ROWS, width), F32),
        compiler_params=_params(2),
        name="ada",
    )(cond, w_mod, b_mod.reshape(depth, 1, width))


def _rms_mod(x, gain, shift, scale):
    ms = jnp.mean(x * x, axis=-1, keepdims=True)
    return x * lax.rsqrt(ms + EPS) * (gain * (1.0 + scale)) + shift


def _ffn_kernel(*refs, premix, emit_h, ffn_idx):
    x_ref, mod_ref, gain_ref = refs[:3]
    refs = refs[3:]
    if premix:
        t_ref, wt_ref = refs[:2]
        refs = refs[2:]
    wgu_ref, wd_ref, out_ref = refs[:3]
    ks = 6 if ffn_idx else 0
    grow = 2 if ffn_idx else 0
    sub = min(FFN_SUBTILE, x_ref.shape[1])

    def chain(st):
        rows = slice(st * sub, (st + 1) * sub)
        x = x_ref[0, rows, :]
        if premix:
            x = x + mod_ref[0, 5:6, :] * jnp.dot(t_ref[0, rows, :], wt_ref[...],
                                                 preferred_element_type=F32)
        h = _rms_mod(x, gain_ref[grow:grow + 1, :], mod_ref[0, ks:ks + 1, :],
                     mod_ref[0, ks + 1:ks + 2, :]).astype(BF16)
        yield
        acc = jnp.zeros(x.shape, F32)
        for c in range(D_FF // FFN_CHUNK):
            lo = c * FFN_CHUNK
            g = jnp.dot(h, wgu_ref[:, lo:lo + FFN_CHUNK], preferred_element_type=F32)
            u = jnp.dot(h, wgu_ref[:, D_FF + lo:D_FF + lo + FFN_CHUNK], preferred_element_type=F32)
            a = (g * jax.nn.sigmoid(g) * u).astype(BF16)
            acc = acc + jnp.dot(a, wd_ref[lo:lo + FFN_CHUNK, :], preferred_element_type=F32)
            yield
        out = x + (0.5 * mod_ref[0, ks + 2:ks + 3, :]) * acc
        out_ref[0, rows, :] = out
        if emit_h:
            refs[3][0, rows, :] = _rms_mod(out, gain_ref[1:2, :], mod_ref[0, 3:4, :],
                                           mod_ref[0, 4:5, :]).astype(BF16)

    _trace_staggered([chain(st) for st in range(x_ref.shape[1] // sub)], FFN_STAGGER)


def _resident_layer(shape, layer):
    tail = (0,) * (len(shape) - 1)
    return pl.BlockSpec((None,) + tuple(shape[1:]), lambda *_: (layer,) + tail,
                        pipeline_mode=pl.Buffered(1))


def _ffn_call(x, mod, gains, w_gu, w_d, layer, *, ffn_idx, mix=None, emit_h=False):
    b, t, d = x.shape
    tm = min(FFN_TOKENS, t)
    tok = pl.BlockSpec((1, tm, d), lambda i, j: (i, j, 0))
    mod_map = (lambda i, j: (i, 0, 0)) if mod.shape[0] == b else (lambda i, j: (0, 0, 0))
    in_specs = [tok, pl.BlockSpec((1, N_MOD, d), mod_map), _resident(gains.shape)]
    args = [x, mod, gains]
    if mix is not None:
        in_specs += [tok, _resident(mix[1].shape)]
        args += [mix[0], mix[1]]
    in_specs += [_resident_layer(w_gu.shape, layer), _resident_layer(w_d.shape, layer)]
    args += [w_gu, w_d]
    out_shape = [jax.ShapeDtypeStruct(x.shape, F32)]
    out_specs = [tok]
    if emit_h:
        out_shape.append(jax.ShapeDtypeStruct(x.shape, BF16))
        out_specs.append(tok)
    res = pl.pallas_call(
        functools.partial(_ffn_kernel, premix=mix is not None, emit_h=emit_h, ffn_idx=ffn_idx),
        grid=(b, t // tm),
        in_specs=in_specs,
        out_specs=out_specs,
        out_shape=out_shape,
        compiler_params=_params(2),
        name="ffn",
    )(*args)
    return res if emit_h else res[0]


def _swap_rotary_halves(t):
    return jnp.concatenate([t[..., 16:32, :], t[..., 0:16, :], t[..., 48:64, :], t[..., 32:48, :]],
                           axis=-2)


def _qkv_kernel(h_ref, wt_ref, gq_ref, gk_ref, cos_ref, sin_ref, *out_refs, rope, q_scale):
    h = h_ref[0]
    tm = h.shape[0]
    groups = D_MODEL // HEAD_DIM

    def project(lo):
        return lax.dot_general(wt_ref[lo:lo + D_MODEL, :], h, (((1,), (1,)), ((), ())),
                               preferred_element_type=F32)

    def normed(t, g_ref, scale):
        t = t.reshape(groups, HEAD_DIM, tm)
        inv = lax.rsqrt(jnp.sum(t * t, axis=1, keepdims=True) * (1.0 / HEAD_DIM) + EPS)
        gain = jnp.concatenate([g_ref[...] * scale] * (tm // LANES), axis=1)
        if not rope:
            return (t * inv * gain).reshape(D_MODEL, tm)
        cos_g = cos_ref[...] * gain
        sin_g = sin_ref[...] * _swap_rotary_halves(gain)
        return (inv * (t * cos_g + _swap_rotary_halves(t) * sin_g)).reshape(D_MODEL, tm)

    if rope:
        qt_ref, k_ref, vt_ref = out_refs
        qt_ref[0] = normed(project(0), gq_ref, q_scale).astype(BF16)
    else:
        k_ref, vt_ref = out_refs
    k_ref[0] = normed(project(D_MODEL), gk_ref, 1.0).T.astype(BF16)
    vt_ref[0] = project(2 * D_MODEL).astype(BF16)


def _qkv_call(h, wt_qkv, gq, gk, cos_t, sin_t, *, rope, q_scale):
    b, t, d = h.shape
    tm = min(QKV_TOKENS, t)
    tok = pl.BlockSpec((1, tm, d), lambda i, j: (i, j, 0))
    tok_t = pl.BlockSpec((1, d, tm), lambda i, j: (i, 0, j))
    tab = pl.BlockSpec((HEAD_DIM, tm), lambda i, j: (0, j))
    shape_t = jax.ShapeDtypeStruct((b, d, t), BF16)
    return pl.pallas_call(
        functools.partial(_qkv_kernel, rope=rope, q_scale=q_scale),
        grid=(b, t // tm),
        in_specs=[tok, _resident(wt_qkv.shape), _resident(gq.shape), _resident(gk.shape), tab, tab],
        out_specs=([tok_t] if rope else []) + [tok, tok_t],
        out_shape=([shape_t] if rope else []) + [jax.ShapeDtypeStruct((b, t, d), BF16), shape_t],
        compiler_params=_params(2),
        name="qkv",
    )(h, wt_qkv, gq, gk, cos_t, sin_t)


def _attn_kernel(q_ref, kl_ref, kc_ref, vl_ref, vc_ref, lq1_ref, lk1_ref, lq2_ref, lk2_ref,
                 subg_ref, gq_ref, gk_ref, o_ref, flag_ref, *, lambda_init, q_scale):
    @pl.when((pl.program_id(0) == 0) & (pl.program_id(1) == 0) & (pl.program_id(2) == 0))
    def _():
        bound = ((HEAD_DIM * q_scale * 1.05) * jnp.max(jnp.abs(gq_ref[...]))
                 * jnp.max(jnp.abs(gk_ref[...])))
        flag_ref[0] = (bound < EXP2_SAFE).astype(jnp.int32)

    shift_free = flag_ref[0] == 1

    def chain(st, sub, subtract_max, lam):
        cols = slice(st * sub, (st + 1) * sub)
        qt = q_ref[0, :, cols]
        dim = lax.broadcasted_iota(jnp.int32, qt.shape, 0)
        outs = []
        for comp in range(2):
            sel = (dim < HEAD_DIM) if comp == 0 else (dim >= HEAD_DIM)
            qm = jnp.where(sel, qt, jnp.zeros_like(qt))
            s_c = jnp.dot(kc_ref[0], qm, preferred_element_type=F32)
            s_l = jnp.dot(kl_ref[0], qm, preferred_element_type=F32)
            if subtract_max:
                m = jnp.maximum(jnp.max(s_c, axis=0, keepdims=True),
                                jnp.max(s_l, axis=0, keepdims=True))
                s_c, s_l = s_c - m, s_l - m
            e_c, e_l = jnp.exp2(s_c), jnp.exp2(s_l)
            yield
            r = (jnp.dot(vc_ref[0], e_c.astype(BF16), preferred_element_type=F32)
                 + jnp.dot(vl_ref[0], e_l.astype(BF16), preferred_element_type=F32))
            total = jnp.sum(e_c, axis=0, keepdims=True) + jnp.sum(e_l, axis=0, keepdims=True)
            outs.append(r / total)
            if comp == 0:
                yield
        o = outs[0] - lam * outs[1]
        ms = jnp.mean(o * o, axis=0, keepdims=True)
        o = o * lax.rsqrt(ms + EPS) * (subg_ref[...] * (1.0 - lambda_init))
        o_ref[0, cols, :] = o.T.astype(BF16)

    def attend(subtract_max):
        lam = (jnp.exp(jnp.sum(lq1_ref[...] * lk1_ref[...], axis=-1, keepdims=True))
               - jnp.exp(jnp.sum(lq2_ref[...] * lk2_ref[...], axis=-1, keepdims=True))
               + lambda_init)
        tq = q_ref.shape[2]
        sub = tq if subtract_max else min(ATTN_SUBTILE, tq)
        _trace_staggered([chain(st, sub, subtract_max, lam) for st in range(tq // sub)], 1)

    pl.when(shift_free)(functools.partial(attend, False))
    pl.when(jnp.logical_not(shift_free))(functools.partial(attend, True))


def _attn_call(qt, k_lat, vt_lat, k_ctx, vt_ctx, lq1, lk1, lq2, lk2, sub_g, gq, gk, *,
               lambda_init, q_scale):
    b, d, s = qt.shape
    n_ctx = k_ctx.shape[1]
    tq = ATTN_Q_TILE
    vec = pl.BlockSpec((1, HEAD_DIM), lambda i, h, j: (0, 0))
    gain = pl.BlockSpec(gq.shape, lambda i, h, j: (0, 0))
    return pl.pallas_call(
        functools.partial(_attn_kernel, lambda_init=lambda_init, q_scale=q_scale),
        grid=(b, N_HEADS, s // tq),
        in_specs=[pl.BlockSpec((1, V_DIM, tq), lambda i, h, j: (i, h, j)),
                  pl.BlockSpec((1, s, V_DIM), lambda i, h, j: (i, 0, h)),
                  pl.BlockSpec((1, n_ctx, V_DIM), lambda i, h, j: (i, 0, h)),
                  pl.BlockSpec((1, V_DIM, s), lambda i, h, j: (i, h, 0)),
                  pl.BlockSpec((1, V_DIM, n_ctx), lambda i, h, j: (i, h, 0)),
                  vec, vec, vec, vec, pl.BlockSpec((V_DIM, 1), lambda i, h, j: (0, 0)), gain, gain],
        out_specs=pl.BlockSpec((1, tq, V_DIM), lambda i, h, j: (i, j, h)),
        out_shape=jax.ShapeDtypeStruct((b, s, d), BF16),
        scratch_shapes=[pltpu.SMEM((1,), jnp.int32)],
        compiler_params=_params(3),
        name="attn",
    )(qt, k_lat, k_ctx, vt_lat, vt_ctx, lq1, lk1, lq2, lk2, sub_g.reshape(V_DIM, 1), gq, gk)


def _fourier_kernel(h_ref, w1_ref, tab_ref, o_ref, w1_bf, tab_bf, z_s):
    @pl.when((pl.program_id(0) == 0) & (pl.program_id(1) == 0))
    def _():
        w1_bf[...] = w1_ref[...].astype(BF16)
        for i in range(tab_ref.shape[0]):
            tab_bf[i] = tab_ref[i].astype(BF16)

    z = jnp.dot(h_ref[0], w1_bf[...], preferred_element_type=F32)
    for c in range(z_s.shape[0]):
        z_s[c] = z[:, c * LANES:(c + 1) * LANES]
    n2 = z_s.shape[1] // 2

    def rows(parity):
        return jnp.concatenate([z_s[c, pl.ds(parity, n2, stride=2), :].astype(BF16)
                                for c in range(z_s.shape[0])], axis=1)

    ze = rows(0)
    zo = rows(1)
    m = FOURIER_GROUP
    e = (jnp.dot(tab_bf[0], ze[:, :m], preferred_element_type=F32)
         + jnp.dot(tab_bf[1], ze[:, m:], preferred_element_type=F32))
    o = (jnp.dot(tab_bf[2], zo[:, :m], preferred_element_type=F32)
         + jnp.dot(tab_bf[3], zo[:, m:], preferred_element_type=F32))
    o_ref[0, :n2, :] = (e + o).astype(BF16)
    o_ref[0, n2:, :] = (e - o).astype(BF16)


def _fourier_call(h, w1, tab):
    b, s, d = h.shape
    blk = pl.BlockSpec((1, s, FOURIER_GROUP), lambda i, g: (i, 0, g))
    return pl.pallas_call(
        _fourier_kernel,
        grid=(b, d // FOURIER_GROUP),
        in_specs=[blk, _resident(w1.shape), _resident(tab.shape)],
        out_specs=blk,
        out_shape=jax.ShapeDtypeStruct(h.shape, BF16),
        scratch_shapes=[pltpu.VMEM(w1.shape, BF16), pltpu.VMEM(tab.shape, BF16),
                        pltpu.VMEM((2 * FOURIER_GROUP // LANES, s, LANES), F32)],
        compiler_params=_params(2),
        name="fourier",
    )(h, w1, tab)


def _dft_tables(n_pos):
    m = FOURIER_GROUP
    km = (np.outer(np.arange(m), np.arange(m)) % m) * (2.0 * np.pi / m)
    w1 = np.concatenate([np.cos(km), np.sin(km)], axis=1) / math.sqrt(m)
    k = np.arange(n_pos // 2)
    tabs = []
    for parity in range(2):
        kn = (np.outer(k, 2 * k + parity) % n_pos) * (2.0 * np.pi / n_pos)
        tabs += [np.cos(kn) / math.sqrt(n_pos), -np.sin(kn) / math.sqrt(n_pos)]
    return jnp.asarray(w1, F32), jnp.asarray(np.stack(tabs), F32)


def _rope_tables(n_pos):
    pos = np.arange(n_pos)
    inv_freq = ROPE_THETA ** (-(np.arange(ROPE_PAIRS, dtype=np.float64) / ROPE_PAIRS))
    dim = np.arange(HEAD_DIM)
    coord = np.where((dim // 32)[:, None] == 0, (pos // GRID_W)[None, :], (pos % GRID_W)[None, :])
    ang = coord * inv_freq[dim % ROPE_PAIRS][:, None]
    sign = np.where((dim % 32) < 16, -1.0, 1.0)[:, None]
    return jnp.asarray(np.cos(ang), F32), jnp.asarray(np.sin(ang) * sign, F32)


def kernel(x, c, ctx, c_ctx, norm_g, w_mod, b_mod, ffn1_w_gu, ffn1_w_d, ffn2_w_gu, ffn2_w_d,
           attn_w_qkv, attn_w_o, attn_q_g, attn_k_g, attn_lam_q1, attn_lam_k1, attn_lam_q2,
           attn_lam_k2, attn_sub_g, fourier_w):
    b, s, d = x.shape
    cond = jnp.concatenate([c, c_ctx[None, :], jnp.zeros((COND_ROWS - b - 1, d), F32)], axis=0)
    mods = _ada_call(cond, w_mod, b_mod)
    mx = [mods[i, :b].reshape(b, N_MOD, d) for i in range(2)]
    my0 = mods[0, b:b + 1].reshape(1, N_MOD, d)

    wgu1, wd1 = ffn1_w_gu.astype(BF16), ffn1_w_d.astype(BF16)
    wgu2, wd2 = ffn2_w_gu.astype(BF16), ffn2_w_d.astype(BF16)
    wt_qkv = attn_w_qkv[0].T.astype(BF16)
    w_o = attn_w_o[0].astype(BF16)
    w_f = fourier_w[0].astype(BF16)

    x, hx = _ffn_call(x, mx[0], norm_g[0], wgu1, wd1, 0, ffn_idx=0, emit_h=True)
    n_ctx = ctx.shape[1]
    ctx_rows = ctx.reshape(b * n_ctx // FFN_TOKENS, FFN_TOKENS, d)
    _, hy = _ffn_call(ctx_rows, my0, norm_g[0], wgu1, wd1, 0, ffn_idx=0, emit_h=True)
    cos_t, sin_t = _rope_tables(s)
    gq = jnp.broadcast_to(attn_q_g[0][:, None], (HEAD_DIM, LANES))
    gk = jnp.broadcast_to(attn_k_g[0][:, None], (HEAD_DIM, LANES))
    q_scale = math.log2(math.e) / math.sqrt(HEAD_DIM)
    qt, k, vt = _qkv_call(hx, wt_qkv, gq, gk, cos_t, sin_t, rope=True, q_scale=q_scale)
    ky, vyt = _qkv_call(hy.reshape(b, n_ctx, d), wt_qkv, gq, gk, cos_t, sin_t, rope=False,
                        q_scale=1.0)
    lambda_init = 0.8 - 0.6 * math.exp(-0.3 * 0)
    o = _attn_call(qt, k, vt, ky, vyt, attn_lam_q1[0:1], attn_lam_k1[0:1], attn_lam_q2[0:1],
                   attn_lam_k2[0:1], attn_sub_g[0:1], gq, gk, lambda_init=lambda_init,
                   q_scale=q_scale)
    x = _ffn_call(x, mx[0], norm_g[0], wgu2, wd2, 0, ffn_idx=1, mix=(o, w_o))

    x, hx = _ffn_call(x, mx[1], norm_g[1], wgu1, wd1, 1, ffn_idx=0, emit_h=True)
    f = _fourier_call(hx, *_dft_tables(s))
    x = _ffn_call(x, mx[1], norm_g[1], wgu2, wd2, 1, ffn_idx=1, mix=(f, w_f))
    return x
```

```python
import functools
import math

import numpy as np
import jax
import jax.numpy as jnp
from jax import lax
from jax.experimental import pallas as pl
from jax.experimental.pallas import tpu as pltpu

F32 = jnp.float32
BF16 = jnp.bfloat16

D_MODEL = 1024
N_MOD = 9
N_HEADS = 8
HEAD_DIM = 64
V_DIM = 2 * HEAD_DIM
GRID_W = 64
ROPE_THETA = 10000.0
ROPE_PAIRS = HEAD_DIM // 4
N_FOURIER_GROUPS = 4
FOURIER_GROUP = D_MODEL // N_FOURIER_GROUPS
D_FF = 2816
EPS = 1e-6
EXP2_SAFE = 100.0

LANES = 128
MXU_DIM = 256
COND_ROWS = 24
VMEM_LIMIT = 56 * 1024 * 1024

FFN_TOKENS = 1024
FFN_SUBTILE = 512
FFN_STAGGER = 2
FFN_CHUNK = 256
QKV_TOKENS = 1024
QKV_FEATURE_BLOCK = 256
ATTN_Q_TILE = 1024
ADA_COLS = 1024


def _resident(shape):
    nd = len(shape)
    return pl.BlockSpec(shape, lambda *_: (0,) * nd, pipeline_mode=pl.Buffered(1))


def _params(n_axes):
    return pltpu.CompilerParams(dimension_semantics=("arbitrary",) * n_axes,
                                vmem_limit_bytes=VMEM_LIMIT)


def _trace_staggered(chains, stagger):
    started, live, step = 0, [], 0
    while live or started < len(chains):
        if started < len(chains) and step % stagger == 0:
            live.append(chains[started])
            started += 1
        for ch in list(live):
            if next(ch, StopIteration) is StopIteration:
                live.remove(ch)
        step += 1


def _ada_kernel(cond_ref, w_ref, b_ref, o_ref):
    cnd = cond_ref[...]
    act = (cnd * jax.nn.sigmoid(cnd)).astype(BF16)
    o_ref[0] = jnp.dot(act, w_ref[0].astype(BF16), preferred_element_type=F32) + b_ref[0]


def _ada_call(cond, w_mod, b_mod):
    depth, _, width = w_mod.shape
    return pl.pallas_call(
        _ada_kernel,
        grid=(depth, width // ADA_COLS),
        in_specs=[
            pl.BlockSpec((COND_ROWS, D_MODEL), lambda i, j: (0, 0)),
            pl.BlockSpec((1, D_MODEL, ADA_COLS), lambda i, j: (i, 0, j)),
            pl.BlockSpec((1, 1, ADA_COLS), lambda i, j: (i, 0, j)),
        ],
        out_specs=pl.BlockSpec((1, COND_ROWS, ADA_COLS), lambda i, j: (i, 0, j)),
        out_shape=jax.ShapeDtypeStruct((depth, COND_ROWS, width), F32),
        compiler_params=_params(2),
        name="ada",
    )(cond, w_mod, b_mod.reshape(depth, 1, width))


def _rms_mod(x, gain, shift, scale):
    ms = jnp.mean(x * x, axis=-1, keepdims=True)
    return x * lax.rsqrt(ms + EPS) * (gain * (1.0 + scale)) + shift


def _ffn_kernel(*refs, premix, emit_h, ffn_idx):
    x_ref, mod_ref, gain_ref = refs[:3]
    refs = refs[3:]
    if premix:
        t_ref, wt_ref = refs[:2]
        refs = refs[2:]
    wgu_ref, wd_ref, out_ref = refs[:3]
    ks = 6 if ffn_idx else 0
    grow = 2 if ffn_idx else 0
    sub = min(FFN_SUBTILE, x_ref.shape[1])

    def chain(st):
        rows = slice(st * sub, (st + 1) * sub)
        x = x_ref[0, rows, :]
        if premix:
            x = x + mod_ref[0, 5:6, :] * jnp.dot(t_ref[0, rows, :], wt_ref[...],
                                                 preferred_element_type=F32)
        h = _rms_mod(x, gain_ref[grow:grow + 1, :], mod_ref[0, ks:ks + 1, :],
                     mod_ref[0, ks + 1:ks + 2, :]).astype(BF16)
        yield
        acc = jnp.zeros(x.shape, F32)
        for c in range(D_FF // FFN_CHUNK):
            lo = c * FFN_CHUNK
            g = jnp.dot(h, wgu_ref[:, lo:lo + FFN_CHUNK], preferred_element_type=F32)
            u = jnp.dot(h, wgu_ref[:, D_FF + lo:D_FF + lo + FFN_CHUNK], preferred_element_type=F32)
            a = (g * jax.nn.sigmoid(g) * u).astype(BF16)
            acc = acc + jnp.dot(a, wd_ref[lo:lo + FFN_CHUNK, :], preferred_element_type=F32)
            yield
        out = x + (0.5 * mod_ref[0, ks + 2:ks + 3, :]) * acc
        out_ref[0, rows, :] = out
        if emit_h:
            refs[3][0, rows, :] = _rms_mod(out, gain_ref[1:2, :], mod_ref[0, 3:4, :],
                                           mod_ref[0, 4:5, :]).astype(BF16)

    _trace_staggered([chain(st) for st in range(x_ref.shape[1] // sub)], FFN_STAGGER)


def _resident_layer(shape, layer):
    tail = (0,) * (len(shape) - 1)
    return pl.BlockSpec((None,) + tuple(shape[1:]), lambda *_: (layer,) + tail,
                        pipeline_mode=pl.Buffered(1))


def _ffn_call(x, mod, gains, w_gu, w_d, layer, *, ffn_idx, mix=None, emit_h=False):
    b, t, d = x.shape
    tm = min(FFN_TOKENS, t)
    tok = pl.BlockSpec((1, tm, d), lambda i, j: (i, j, 0))
    mod_map = (lambda i, j: (i, 0, 0)) if mod.shape[0] == b else (lambda i, j: (0, 0, 0))
    in_specs = [tok, pl.BlockSpec((1, N_MOD, d), mod_map), _resident(gains.shape)]
    args = [x, mod, gains]
    if mix is not None:
        in_specs += [tok, _resident(mix[1].shape)]
        args += [mix[0], mix[1]]
    in_specs += [_resident_layer(w_gu.shape, layer), _resident_layer(w_d.shape, layer)]
    args += [w_gu, w_d]
    out_shape = [jax.ShapeDtypeStruct(x.shape, F32)]
    out_specs = [tok]
    if emit_h:
        out_shape.append(jax.ShapeDtypeStruct(x.shape, BF16))
        out_specs.append(tok)
    res = pl.pallas_call(
        functools.partial(_ffn_kernel, premix=mix is not None, emit_h=emit_h, ffn_idx=ffn_idx),
        grid=(b, t // tm),
        in_specs=in_specs,
        out_specs=out_specs,
        out_shape=out_shape,
        compiler_params=_params(2),
        name="ffn",
    )(*args)
    return res if emit_h else res[0]


def _swap_rotary_halves(t):
    return jnp.concatenate([t[..., 16:32, :], t[..., 0:16, :], t[..., 48:64, :], t[..., 32:48, :]],
                           axis=-2)


def _qkv_kernel(h_ref, wt_ref, gq_ref, gk_ref, cos_ref, sin_ref, *out_refs, rope, q_scale):
    h = h_ref[0]
    tm = h.shape[0]
    blk = QKV_FEATURE_BLOCK
    groups = blk // HEAD_DIM

    def project(lo):
        return lax.dot_general(wt_ref[lo:lo + blk, :], h, (((1,), (1,)), ((), ())),
                               preferred_element_type=F32)

    def gains(g_ref, scale):
        gain = jnp.concatenate([g_ref[...] * scale] * (tm // LANES), axis=1)
        if not rope:
            return gain, None
        return cos_ref[...] * gain, sin_ref[...] * _swap_rotary_halves(gain)

    def normed(t, cos_g, sin_g):
        t = t.reshape(groups, HEAD_DIM, tm)
        inv = lax.rsqrt(jnp.sum(t * t, axis=1, keepdims=True) * (1.0 / HEAD_DIM) + EPS)
        if sin_g is None:
            return (t * inv * cos_g).reshape(blk, tm)
        return (inv * (t * cos_g + _swap_rotary_halves(t) * sin_g)).reshape(blk, tm)

    if rope:
        qt_ref, k_ref, vt_ref = out_refs
        cos_q, sin_q = gains(gq_ref, q_scale)
        for lo in range(0, D_MODEL, blk):
            qt_ref[0, lo:lo + blk, :] = normed(project(lo), cos_q, sin_q).astype(BF16)
    else:
        k_ref, vt_ref = out_refs
    cos_k, sin_k = gains(gk_ref, 1.0)
    for lo in range(0, D_MODEL, blk):
        k_ref[0, :, lo:lo + blk] = normed(project(D_MODEL + lo), cos_k, sin_k).T.astype(BF16)
    for lo in range(0, D_MODEL, blk):
        vt_ref[0, lo:lo + blk, :] = project(2 * D_MODEL + lo).astype(BF16)


def _qkv_call(h, wt_qkv, gq, gk, cos_t, sin_t, *, rope, q_scale):
    b, t, d = h.shape
    tm = min(QKV_TOKENS, t)
    tok = pl.BlockSpec((1, tm, d), lambda i, j: (i, j, 0))
    tok_t = pl.BlockSpec((1, d, tm), lambda i, j: (i, 0, j))
    tab = pl.BlockSpec((HEAD_DIM, tm), lambda i, j: (0, j))
    shape_t = jax.ShapeDtypeStruct((b, d, t), BF16)
    return pl.pallas_call(
        functools.partial(_qkv_kernel, rope=rope, q_scale=q_scale),
        grid=(b, t // tm),
        in_specs=[tok, _resident(wt_qkv.shape), _resident(gq.shape), _resident(gk.shape), tab, tab],
        out_specs=([tok_t] if rope else []) + [tok, tok_t],
        out_shape=([shape_t] if rope else []) + [jax.ShapeDtypeStruct((b, t, d), BF16), shape_t],
        compiler_params=_params(2),
        name="qkv",
    )(h, wt_qkv, gq, gk, cos_t, sin_t)


def _attn_kernel(q_ref, kl_ref, kc_ref, vl_ref, vc_ref, lq1_ref, lk1_ref, lq2_ref, lk2_ref,
                 subg_ref, gq_ref, gk_ref, o_ref, flag_ref, *, lambda_init, q_scale):
    @pl.when((pl.program_id(0) == 0) & (pl.program_id(1) == 0) & (pl.program_id(2) == 0))
    def _():
        bound = ((HEAD_DIM * q_scale * 1.05) * jnp.max(jnp.abs(gq_ref[...]))
                 * jnp.max(jnp.abs(gk_ref[...])))
        flag_ref[0] = (bound < EXP2_SAFE).astype(jnp.int32)

    shift_free = flag_ref[0] == 1

    def attend(subtract_max):
        lam = (jnp.exp(jnp.sum(lq1_ref[...] * lk1_ref[...], axis=-1, keepdims=True))
               - jnp.exp(jnp.sum(lq2_ref[...] * lk2_ref[...], axis=-1, keepdims=True))
               + lambda_init)
        qt = q_ref[0]
        dim = lax.broadcasted_iota(jnp.int32, qt.shape, 0)
        outs = []
        for comp in range(2):
            sel = (dim < HEAD_DIM) if comp == 0 else (dim >= HEAD_DIM)
            qm = jnp.where(sel, qt, jnp.zeros_like(qt))
            s_c = jnp.dot(kc_ref[0], qm, preferred_element_type=F32)
            s_l = jnp.dot(kl_ref[0], qm, preferred_element_type=F32)
            if subtract_max:
                m = jnp.maximum(jnp.max(s_c, axis=0, keepdims=True),
                                jnp.max(s_l, axis=0, keepdims=True))
                s_c, s_l = s_c - m, s_l - m
            e_c, e_l = jnp.exp2(s_c), jnp.exp2(s_l)
            r = (jnp.dot(vc_ref[0], e_c.astype(BF16), preferred_element_type=F32)
                 + jnp.dot(vl_ref[0], e_l.astype(BF16), preferred_element_type=F32))
            total = jnp.sum(e_c, axis=0, keepdims=True) + jnp.sum(e_l, axis=0, keepdims=True)
            outs.append(r / total)
        o = outs[0] - lam * outs[1]
        ms = jnp.mean(o * o, axis=0, keepdims=True)
        o = o * lax.rsqrt(ms + EPS) * (subg_ref[...] * (1.0 - lambda_init))
        o_ref[0] = o.T.astype(BF16)

    pl.when(shift_free)(functools.partial(attend, False))
    pl.when(jnp.logical_not(shift_free))(functools.partial(attend, True))


def _attn_call(qt, k_lat, vt_lat, k_ctx, vt_ctx, lq1, lk1, lq2, lk2, sub_g, gq, gk, *,
               lambda_init, q_scale):
    b, d, s = qt.shape
    n_ctx = k_ctx.shape[1]
    tq = ATTN_Q_TILE
    vec = pl.BlockSpec((1, HEAD_DIM), lambda i, h, j: (0, 0))
    gain = pl.BlockSpec(gq.shape, lambda i, h, j: (0, 0))
    return pl.pallas_call(
        functools.partial(_attn_kernel, lambda_init=lambda_init, q_scale=q_scale),
        grid=(b, N_HEADS, s // tq),
        in_specs=[pl.BlockSpec((1, V_DIM, tq), lambda i, h, j: (i, h, j)),
                  pl.BlockSpec((1, s, V_DIM), lambda i, h, j: (i, 0, h)),
                  pl.BlockSpec((1, n_ctx, V_DIM), lambda i, h, j: (i, 0, h)),
                  pl.BlockSpec((1, V_DIM, s), lambda i, h, j: (i, h, 0)),
                  pl.BlockSpec((1, V_DIM, n_ctx), lambda i, h, j: (i, h, 0)),
                  vec, vec, vec, vec, pl.BlockSpec((V_DIM, 1), lambda i, h, j: (0, 0)), gain, gain],
        out_specs=pl.BlockSpec((1, tq, V_DIM), lambda i, h, j: (i, j, h)),
        out_shape=jax.ShapeDtypeStruct((b, s, d), BF16),
        scratch_shapes=[pltpu.SMEM((1,), jnp.int32)],
        compiler_params=_params(3),
        name="attn",
    )(qt, k_lat, k_ctx, vt_lat, vt_ctx, lq1, lk1, lq2, lk2, sub_g.reshape(V_DIM, 1), gq, gk)


def _fourier_kernel(h_ref, w1_ref, tab_ref, o_ref, w1_bf, tab_bf, z_s):
    @pl.when((pl.program_id(0) == 0) & (pl.program_id(1) == 0))
    def _():
        w1_bf[...] = w1_ref[...].astype(BF16)
        for i in range(tab_ref.shape[0]):
            tab_bf[i] = tab_ref[i].astype(BF16)

    z = jnp.dot(h_ref[0], w1_bf[...], preferred_element_type=F32)
    for c in range(z_s.shape[0]):
        z_s[c] = z[:, c * LANES:(c + 1) * LANES]
    n2 = z_s.shape[1] // 2

    def rows(parity):
        return jnp.concatenate([z_s[c, pl.ds(parity, n2, stride=2), :].astype(BF16)
                                for c in range(z_s.shape[0])], axis=1)

    ze = rows(0)
    zo = rows(1)
    m = FOURIER_GROUP
    e = (jnp.dot(tab_bf[0], ze[:, :m], preferred_element_type=F32)
         + jnp.dot(tab_bf[1], ze[:, m:], preferred_element_type=F32))
    o = (jnp.dot(tab_bf[2], zo[:, :m], preferred_element_type=F32)
         + jnp.dot(tab_bf[3], zo[:, m:], preferred_element_type=F32))
    o_ref[0, :n2, :] = (e + o).astype(BF16)
    o_ref[0, n2:, :] = (e - o).astype(BF16)


def _fourier_call(h, w1, tab):
    b, s, d = h.shape
    blk = pl.BlockSpec((1, s, FOURIER_GROUP), lambda i, g: (i, 0, g))
    return pl.pallas_call(
        _fourier_kernel,
        grid=(b, d // FOURIER_GROUP),
        in_specs=[blk, _resident(w1.shape), _resident(tab.shape)],
        out_specs=blk,
        out_shape=jax.ShapeDtypeStruct(h.shape, BF16),
        scratch_shapes=[pltpu.VMEM(w1.shape, BF16), pltpu.VMEM(tab.shape, BF16),
                        pltpu.VMEM((2 * FOURIER_GROUP // LANES, s, LANES), F32)],
        compiler_params=_params(2),
        name="fourier",
    )(h, w1, tab)


def _dft_tables(n_pos):
    m = FOURIER_GROUP
    km = (np.outer(np.arange(m), np.arange(m)) % m) * (2.0 * np.pi / m)
    w1 = np.concatenate([np.cos(km), np.sin(km)], axis=1) / math.sqrt(m)
    k = np.arange(n_pos // 2)
    tabs = []
    for parity in range(2):
        kn = (np.outer(k, 2 * k + parity) % n_pos) * (2.0 * np.pi / n_pos)
        tabs += [np.cos(kn) / math.sqrt(n_pos), -np.sin(kn) / math.sqrt(n_pos)]
    return jnp.asarray(w1, F32), jnp.asarray(np.stack(tabs), F32)


def _rope_tables(n_pos):
    pos = np.arange(n_pos)
    inv_freq = ROPE_THETA ** (-(np.arange(ROPE_PAIRS, dtype=np.float64) / ROPE_PAIRS))
    dim = np.arange(HEAD_DIM)
    coord = np.where((dim // 32)[:, None] == 0, (pos // GRID_W)[None, :], (pos % GRID_W)[None, :])
    ang = coord * inv_freq[dim % ROPE_PAIRS][:, None]
    sign = np.where((dim % 32) < 16, -1.0, 1.0)[:, None]
    return jnp.asarray(np.cos(ang), F32), jnp.asarray(np.sin(ang) * sign, F32)


def kernel(x, c, ctx, c_ctx, norm_g, w_mod, b_mod, ffn1_w_gu, ffn1_w_d, ffn2_w_gu, ffn2_w_d,
           attn_w_qkv, attn_w_o, attn_q_g, attn_k_g, attn_lam_q1, attn_lam_k1, attn_lam_q2,
           attn_lam_k2, attn_sub_g, fourier_w):
    b, s, d = x.shape
    cond = jnp.concatenate([c, c_ctx[None, :], jnp.zeros((COND_ROWS - b - 1, d), F32)], axis=0)
    mods = _ada_call(cond, w_mod, b_mod)
    mx = [mods[i, :b].reshape(b, N_MOD, d) for i in range(2)]
    my0 = mods[0, b:b + 1].reshape(1, N_MOD, d)

    wgu1, wd1 = ffn1_w_gu.astype(BF16), ffn1_w_d.astype(BF16)
    wgu2, wd2 = ffn2_w_gu.astype(BF16), ffn2_w_d.astype(BF16)
    wt_qkv = attn_w_qkv[0].T.astype(BF16)
    w_o = attn_w_o[0].astype(BF16)
    w_f = fourier_w[0].astype(BF16)

    x, hx = _ffn_call(x, mx[0], norm_g[0], wgu1, wd1, 0, ffn_idx=0, emit_h=True)
    n_ctx = ctx.shape[1]
    ctx_rows = ctx.reshape(b * n_ctx // FFN_TOKENS, FFN_TOKENS, d)
    _, hy = _ffn_call(ctx_rows, my0, norm_g[0], wgu1, wd1, 0, ffn_idx=0, emit_h=True)
    cos_t, sin_t = _rope_tables(s)
    gq = jnp.broadcast_to(attn_q_g[0][:, None], (HEAD_DIM, LANES))
    gk = jnp.broadcast_to(attn_k_g[0][:, None], (HEAD_DIM, LANES))
    q_scale = math.log2(math.e) / math.sqrt(HEAD_DIM)
    qt, k, vt = _qkv_call(hx, wt_qkv, gq, gk, cos_t, sin_t, rope=True, q_scale=q_scale)
    ky, vyt = _qkv_call(hy.reshape(b, n_ctx, d), wt_qkv, gq, gk, cos_t, sin_t, rope=False,
                        q_scale=1.0)
    lambda_init = 0.8 - 0.6 * math.exp(-0.3 * 0)
    o = _attn_call(qt, k, vt, ky, vyt, attn_lam_q1[0:1], attn_lam_k1[0:1], attn_lam_q2[0:1],
                   attn_lam_k2[0:1], attn_sub_g[0:1], gq, gk, lambda_init=lambda_init,
                   q_scale=q_scale)
    x = _ffn_call(x, mx[0], norm_g[0], wgu2, wd2, 0, ffn_idx=1, mix=(o, w_o))

    x, hx = _ffn_call(x, mx[1], norm_g[1], wgu1, wd1, 1, ffn_idx=0, emit_h=True)
    f = _fourier_call(hx, *_dft_tables(s))
    x = _ffn_call(x, mx[1], norm_g[1], wgu2, wd2, 1, ffn_idx=1, mix=(f, w_f))
    return x
```

```python
import functools
import math

import numpy as np
import jax
import jax.numpy as jnp
from jax import lax
from jax.experimental import pallas as pl
from jax.experimental.pallas import tpu as pltpu

F32 = jnp.float32
BF16 = jnp.bfloat16

D_MODEL = 1024
N_MOD = 9
N_HEADS = 8
HEAD_DIM = 64
V_DIM = 2 * HEAD_DIM
GRID_W = 64
ROPE_THETA = 10000.0
ROPE_PAIRS = HEAD_DIM // 4
N_FOURIER_GROUPS = 4
FOURIER_GROUP = D_MODEL // N_FOURIER_GROUPS
D_FF = 2816
EPS = 1e-6
EXP2_SAFE = 100.0

LANES = 128
BOUND_SLACK = 1.05
COND_ROWS = 24
VMEM_LIMIT = 56 * 1024 * 1024

FFN_TOKENS = 1024
FFN_SUBTILE = 512
FFN_STAGGER = 2
FFN_CHUNK = 256
QKV_TOKENS = 1024
QKV_FEATURE_BLOCK = 256
ATTN_Q_TILE = 1024
ADA_COLS = 1024


def _resident(shape):
    nd = len(shape)
    return pl.BlockSpec(shape, lambda *_: (0,) * nd, pipeline_mode=pl.Buffered(1))


def _params(n_axes):
    return pltpu.CompilerParams(dimension_semantics=("arbitrary",) * n_axes,
                                vmem_limit_bytes=VMEM_LIMIT)


def _trace_staggered(chains, stagger):
    started, live, step = 0, [], 0
    while live or started < len(chains):
        if started < len(chains) and step % stagger == 0:
            live.append(chains[started])
            started += 1
        for ch in list(live):
            if next(ch, StopIteration) is StopIteration:
                live.remove(ch)
        step += 1


def _ada_kernel(cond_ref, w_ref, b_ref, o_ref):
    cnd = cond_ref[...]
    act = (cnd * jax.nn.sigmoid(cnd)).astype(BF16)
    o_ref[0] = jnp.dot(act, w_ref[0].astype(BF16), preferred_element_type=F32) + b_ref[0]


def _ada_call(cond, w_mod, b_mod):
    depth, _, width = w_mod.shape
    return pl.pallas_call(
        _ada_kernel,
        grid=(depth, width // ADA_COLS),
        in_specs=[
            pl.BlockSpec((COND_ROWS, D_MODEL), lambda i, j: (0, 0)),
            pl.BlockSpec((1, D_MODEL, ADA_COLS), lambda i, j: (i, 0, j)),
            pl.BlockSpec((1, 1, ADA_COLS), lambda i, j: (i, 0, j)),
        ],
        out_specs=pl.BlockSpec((1, COND_ROWS, ADA_COLS), lambda i, j: (i, 0, j)),
        out_shape=jax.ShapeDtypeStruct((depth, COND_ROWS, width), F32),
        compiler_params=_params(2),
        name="ada",
    )(cond, w_mod, b_mod.reshape(depth, 1, width))


def _rms_mod(x, gain, shift, scale):
    ms = jnp.mean(x * x, axis=-1, keepdims=True)
    return x * lax.rsqrt(ms + EPS) * (gain * (1.0 + scale)) + shift


def _ffn_kernel(*refs, premix, emit_x, emit_h, ffn_idx):
    x_ref, mod_ref, gain_ref = refs[:3]
    refs = refs[3:]
    if premix:
        t_ref, wt_ref = refs[:2]
        refs = refs[2:]
    wgu_ref, wd_ref = refs[:2]
    outs = list(refs[2:])
    out_ref = outs.pop(0) if emit_x else None
    h_ref = outs.pop(0) if emit_h else None
    ks = 6 if ffn_idx else 0
    grow = 2 if ffn_idx else 0
    sub = min(FFN_SUBTILE, x_ref.shape[1])

    def chain(st):
        rows = slice(st * sub, (st + 1) * sub)
        x = x_ref[0, rows, :]
        if premix:
            x = x + mod_ref[0, 5:6, :] * jnp.dot(t_ref[0, rows, :], wt_ref[...],
                                                 preferred_element_type=F32)
        h = _rms_mod(x, gain_ref[grow:grow + 1, :], mod_ref[0, ks:ks + 1, :],
                     mod_ref[0, ks + 1:ks + 2, :]).astype(BF16)
        yield
        acc = jnp.zeros(x.shape, F32)
        for c in range(D_FF // FFN_CHUNK):
            lo = c * FFN_CHUNK
            g = jnp.dot(h, wgu_ref[:, lo:lo + FFN_CHUNK], preferred_element_type=F32)
            u = jnp.dot(h, wgu_ref[:, D_FF + lo:D_FF + lo + FFN_CHUNK], preferred_element_type=F32)
            a = (g * jax.nn.sigmoid(g) * u).astype(BF16)
            acc = acc + jnp.dot(a, wd_ref[lo:lo + FFN_CHUNK, :], preferred_element_type=F32)
            yield
        out = x + (0.5 * mod_ref[0, ks + 2:ks + 3, :]) * acc
        if emit_x:
            out_ref[0, rows, :] = out
        if emit_h:
            h_ref[0, rows, :] = _rms_mod(out, gain_ref[1:2, :], mod_ref[0, 3:4, :],
                                         mod_ref[0, 4:5, :]).astype(BF16)

    _trace_staggered([chain(st) for st in range(x_ref.shape[1] // sub)], FFN_STAGGER)


def _resident_layer(shape, layer):
    tail = (0,) * (len(shape) - 1)
    return pl.BlockSpec((None,) + tuple(shape[1:]), lambda *_: (layer,) + tail,
                        pipeline_mode=pl.Buffered(1))


def _ffn_call(x, mod, gains, w_gu, w_d, layer, *, ffn_idx, mix=None, emit_x=True, emit_h=False):
    b, t, d = x.shape
    tm = min(FFN_TOKENS, t)
    tok = pl.BlockSpec((1, tm, d), lambda i, j: (i, j, 0))
    mod_map = (lambda i, j: (i, 0, 0)) if mod.shape[0] == b else (lambda i, j: (0, 0, 0))
    in_specs = [tok, pl.BlockSpec((1, N_MOD, d), mod_map), _resident(gains.shape)]
    args = [x, mod, gains]
    if mix is not None:
        in_specs += [tok, _resident(mix[1].shape)]
        args += [mix[0], mix[1]]
    in_specs += [_resident_layer(w_gu.shape, layer), _resident_layer(w_d.shape, layer)]
    args += [w_gu, w_d]
    out_shape = ([jax.ShapeDtypeStruct(x.shape, F32)] * emit_x
                 + [jax.ShapeDtypeStruct(x.shape, BF16)] * emit_h)
    res = pl.pallas_call(
        functools.partial(_ffn_kernel, premix=mix is not None, emit_x=emit_x, emit_h=emit_h,
                          ffn_idx=ffn_idx),
        grid=(b, t // tm),
        in_specs=in_specs,
        out_specs=[tok] * len(out_shape),
        out_shape=out_shape,
        compiler_params=_params(2),
        name="ffn",
    )(*args)
    return res if len(res) > 1 else res[0]


def _swap_rotary_halves(t):
    p = ROPE_PAIRS
    return jnp.concatenate([t[..., p:2 * p, :], t[..., 0:p, :], t[..., 3 * p:4 * p, :],
                            t[..., 2 * p:3 * p, :]], axis=-2)


def _qkv_kernel(h_ref, wt_ref, gq_ref, gk_ref, cos_ref, sin_ref, *out_refs, rope, q_scale):
    h = h_ref[0]
    tm = h.shape[0]
    blk = QKV_FEATURE_BLOCK
    groups = blk // HEAD_DIM

    def project(lo):
        return lax.dot_general(wt_ref[lo:lo + blk, :], h, (((1,), (1,)), ((), ())),
                               preferred_element_type=F32)

    def gains(g_ref, scale):
        gain = jnp.concatenate([g_ref[...] * scale] * (tm // LANES), axis=1)
        if not rope:
            return gain, None
        return cos_ref[...] * gain, sin_ref[...] * _swap_rotary_halves(gain)

    def normed(t, cos_g, sin_g):
        t = t.reshape(groups, HEAD_DIM, tm)
        inv = lax.rsqrt(jnp.sum(t * t, axis=1, keepdims=True) * (1.0 / HEAD_DIM) + EPS)
        if sin_g is None:
            return (t * inv * cos_g).reshape(blk, tm)
        return (inv * (t * cos_g + _swap_rotary_halves(t) * sin_g)).reshape(blk, tm)

    if rope:
        qt_ref, k_ref, vt_ref = out_refs
        cos_q, sin_q = gains(gq_ref, q_scale)
        for lo in range(0, D_MODEL, blk):
            qt_ref[0, lo:lo + blk, :] = normed(project(lo), cos_q, sin_q).astype(BF16)
    else:
        k_ref, vt_ref = out_refs
    cos_k, sin_k = gains(gk_ref, 1.0)
    for lo in range(0, D_MODEL, blk):
        k_ref[0, :, lo:lo + blk] = normed(project(D_MODEL + lo), cos_k, sin_k).T.astype(BF16)
    for lo in range(0, D_MODEL, blk):
        vt_ref[0, lo:lo + blk, :] = project(2 * D_MODEL + lo).astype(BF16)


def _qkv_call(h, wt_qkv, gq, gk, cos_t, sin_t, *, rope, q_scale):
    b, t, d = h.shape
    tm = min(QKV_TOKENS, t)
    tok = pl.BlockSpec((1, tm, d), lambda i, j: (i, j, 0))
    tok_t = pl.BlockSpec((1, d, tm), lambda i, j: (i, 0, j))
    tab = pl.BlockSpec((HEAD_DIM, tm), lambda i, j: (0, j))
    shape_t = jax.ShapeDtypeStruct((b, d, t), BF16)
    return pl.pallas_call(
        functools.partial(_qkv_kernel, rope=rope, q_scale=q_scale),
        grid=(b, t // tm),
        in_specs=[tok, _resident(wt_qkv.shape), _resident(gq.shape), _resident(gk.shape), tab, tab],
        out_specs=([tok_t] if rope else []) + [tok, tok_t],
        out_shape=([shape_t] if rope else []) + [jax.ShapeDtypeStruct((b, t, d), BF16), shape_t],
        compiler_params=_params(2),
        name="qkv",
    )(h, wt_qkv, gq, gk, cos_t, sin_t)


def _attn_kernel(q_ref, kl_ref, kc_ref, vl_ref, vc_ref, lq1_ref, lk1_ref, lq2_ref, lk2_ref,
                 subg_ref, gq_ref, gk_ref, o_ref, flag_ref, *, lambda_init, q_scale):
    @pl.when((pl.program_id(0) == 0) & (pl.program_id(1) == 0) & (pl.program_id(2) == 0))
    def _():
        bound = ((HEAD_DIM * q_scale * BOUND_SLACK) * jnp.max(jnp.abs(gq_ref[...]))
                 * jnp.max(jnp.abs(gk_ref[...])))
        flag_ref[0] = (bound < EXP2_SAFE).astype(jnp.int32)

    shift_free = flag_ref[0] == 1

    def attend(subtract_max):
        lam = (jnp.exp(jnp.sum(lq1_ref[...] * lk1_ref[...], axis=-1, keepdims=True))
               - jnp.exp(jnp.sum(lq2_ref[...] * lk2_ref[...], axis=-1, keepdims=True))
               + lambda_init)
        qt = q_ref[0]
        dim = lax.broadcasted_iota(jnp.int32, qt.shape, 0)
        outs = []
        for comp in range(2):
            sel = (dim < HEAD_DIM) if comp == 0 else (dim >= HEAD_DIM)
            qm = jnp.where(sel, qt, jnp.zeros_like(qt))
            s_c = jnp.dot(kc_ref[0], qm, preferred_element_type=F32)
            s_l = jnp.dot(kl_ref[0], qm, preferred_element_type=F32)
            if subtract_max:
                m = jnp.maximum(jnp.max(s_c, axis=0, keepdims=True),
                                jnp.max(s_l, axis=0, keepdims=True))
                s_c, s_l = s_c - m, s_l - m
            e_c, e_l = jnp.exp2(s_c), jnp.exp2(s_l)
            r = (jnp.dot(vc_ref[0], e_c.astype(BF16), preferred_element_type=F32)
                 + jnp.dot(vl_ref[0], e_l.astype(BF16), preferred_element_type=F32))
            total = jnp.sum(e_c, axis=0, keepdims=True) + jnp.sum(e_l, axis=0, keepdims=True)
            outs.append(r / total)
        o = outs[0] - lam * outs[1]
        ms = jnp.mean(o * o, axis=0, keepdims=True)
        o = o * lax.rsqrt(ms + EPS) * (subg_ref[...] * (1.0 - lambda_init))
        o_ref[0] = o.T.astype(BF16)

    pl.when(shift_free)(functools.partial(attend, False))
    pl.when(jnp.logical_not(shift_free))(functools.partial(attend, True))


def _attn_call(qt, k_lat, vt_lat, k_ctx, vt_ctx, lq1, lk1, lq2, lk2, sub_g, gq, gk, *,
               lambda_init, q_scale):
    b, d, s = qt.shape
    n_ctx = k_ctx.shape[1]
    tq = ATTN_Q_TILE
    vec = pl.BlockSpec((1, HEAD_DIM), lambda i, h, j: (0, 0))
    gain = pl.BlockSpec(gq.shape, lambda i, h, j: (0, 0))
    return pl.pallas_call(
        functools.partial(_attn_kernel, lambda_init=lambda_init, q_scale=q_scale),
        grid=(b, N_HEADS, s // tq),
        in_specs=[pl.BlockSpec((1, V_DIM, tq), lambda i, h, j: (i, h, j)),
                  pl.BlockSpec((1, s, V_DIM), lambda i, h, j: (i, 0, h)),
                  pl.BlockSpec((1, n_ctx, V_DIM), lambda i, h, j: (i, 0, h)),
                  pl.BlockSpec((1, V_DIM, s), lambda i, h, j: (i, h, 0)),
                  pl.BlockSpec((1, V_DIM, n_ctx), lambda i, h, j: (i, h, 0)),
                  vec, vec, vec, vec, pl.BlockSpec((V_DIM, 1), lambda i, h, j: (0, 0)), gain, gain],
        out_specs=pl.BlockSpec((1, tq, V_DIM), lambda i, h, j: (i, j, h)),
        out_shape=jax.ShapeDtypeStruct((b, s, d), BF16),
        scratch_shapes=[pltpu.SMEM((1,), jnp.int32)],
        compiler_params=_params(3),
        name="attn",
    )(qt, k_lat, k_ctx, vt_lat, vt_ctx, lq1, lk1, lq2, lk2, sub_g.reshape(V_DIM, 1), gq, gk)


def _fourier_kernel(h_ref, w1_ref, tab_ref, o_ref, w1_bf, tab_bf, z_s):
    @pl.when((pl.program_id(0) == 0) & (pl.program_id(1) == 0))
    def _():
        w1_bf[...] = w1_ref[...].astype(BF16)
        for i in range(tab_ref.shape[0]):
            tab_bf[i] = tab_ref[i].astype(BF16)

    z = jnp.dot(h_ref[0], w1_bf[...], preferred_element_type=F32)
    for c in range(z_s.shape[0]):
        z_s[c] = z[:, c * LANES:(c + 1) * LANES]
    n2 = z_s.shape[1] // 2

    def rows(parity):
        return jnp.concatenate([z_s[c, pl.ds(parity, n2, stride=2), :].astype(BF16)
                                for c in range(z_s.shape[0])], axis=1)

    ze = rows(0)
    zo = rows(1)
    m = FOURIER_GROUP
    e = (jnp.dot(tab_bf[0], ze[:, :m], preferred_element_type=F32)
         + jnp.dot(tab_bf[1], ze[:, m:], preferred_element_type=F32))
    o = (jnp.dot(tab_bf[2], zo[:, :m], preferred_element_type=F32)
         + jnp.dot(tab_bf[3], zo[:, m:], preferred_element_type=F32))
    o_ref[0, :n2, :] = (e + o).astype(BF16)
    o_ref[0, n2:, :] = (e - o).astype(BF16)


def _fourier_call(h, w1, tab):
    b, s, d = h.shape
    blk = pl.BlockSpec((1, s, FOURIER_GROUP), lambda i, g: (i, 0, g))
    return pl.pallas_call(
        _fourier_kernel,
        grid=(b, d // FOURIER_GROUP),
        in_specs=[blk, _resident(w1.shape), _resident(tab.shape)],
        out_specs=blk,
        out_shape=jax.ShapeDtypeStruct(h.shape, BF16),
        scratch_shapes=[pltpu.VMEM(w1.shape, BF16), pltpu.VMEM(tab.shape, BF16),
                        pltpu.VMEM((2 * FOURIER_GROUP // LANES, s, LANES), F32)],
        compiler_params=_params(2),
        name="fourier",
    )(h, w1, tab)


def _dft_tables(n_pos):
    m = FOURIER_GROUP
    km = (np.outer(np.arange(m), np.arange(m)) % m) * (2.0 * np.pi / m)
    w1 = np.concatenate([np.cos(km), np.sin(km)], axis=1) / math.sqrt(m)
    k = np.arange(n_pos // 2)
    tabs = []
    for parity in range(2):
        kn = (np.outer(k, 2 * k + parity) % n_pos) * (2.0 * np.pi / n_pos)
        tabs += [np.cos(kn) / math.sqrt(n_pos), -np.sin(kn) / math.sqrt(n_pos)]
    return jnp.asarray(w1, F32), jnp.asarray(np.stack(tabs), F32)


def _rope_tables(n_pos):
    pos = np.arange(n_pos)
    inv_freq = ROPE_THETA ** (-(np.arange(ROPE_PAIRS, dtype=np.float64) / ROPE_PAIRS))
    dim = np.arange(HEAD_DIM)
    block = 2 * ROPE_PAIRS
    coord = np.where((dim // block)[:, None] == 0, (pos // GRID_W)[None, :], (pos % GRID_W)[None, :])
    ang = coord * inv_freq[dim % ROPE_PAIRS][:, None]
    sign = np.where((dim % block) < ROPE_PAIRS, -1.0, 1.0)[:, None]
    return jnp.asarray(np.cos(ang), F32), jnp.asarray(np.sin(ang) * sign, F32)


def kernel(x, c, ctx, c_ctx, norm_g, w_mod, b_mod, ffn1_w_gu, ffn1_w_d, ffn2_w_gu, ffn2_w_d,
           attn_w_qkv, attn_w_o, attn_q_g, attn_k_g, attn_lam_q1, attn_lam_k1, attn_lam_q2,
           attn_lam_k2, attn_sub_g, fourier_w):
    b, s, d = x.shape
    cond = jnp.concatenate([c, c_ctx[None, :], jnp.zeros((COND_ROWS - b - 1, d), F32)], axis=0)
    mods = _ada_call(cond, w_mod, b_mod)
    mx = [mods[i, :b].reshape(b, N_MOD, d) for i in range(2)]
    my0 = mods[0, b:b + 1].reshape(1, N_MOD, d)

    wgu1, wd1 = ffn1_w_gu.astype(BF16), ffn1_w_d.astype(BF16)
    wgu2, wd2 = ffn2_w_gu.astype(BF16), ffn2_w_d.astype(BF16)
    wt_qkv = attn_w_qkv[0].T.astype(BF16)
    w_o = attn_w_o[0].astype(BF16)
    w_f = fourier_w[0].astype(BF16)

    x, hx = _ffn_call(x, mx[0], norm_g[0], wgu1, wd1, 0, ffn_idx=0, emit_h=True)
    n_ctx = ctx.shape[1]
    ctx_rows = ctx.reshape(b * n_ctx // FFN_TOKENS, FFN_TOKENS, d)
    hy = _ffn_call(ctx_rows, my0, norm_g[0], wgu1, wd1, 0, ffn_idx=0, emit_x=False, emit_h=True)
    cos_t, sin_t = _rope_tables(s)
    gq = jnp.broadcast_to(attn_q_g[0][:, None], (HEAD_DIM, LANES))
    gk = jnp.broadcast_to(attn_k_g[0][:, None], (HEAD_DIM, LANES))
    q_scale = math.log2(math.e) / math.sqrt(HEAD_DIM)
    qt, k, vt = _qkv_call(hx, wt_qkv, gq, gk, cos_t, sin_t, rope=True, q_scale=q_scale)
    ky, vyt = _qkv_call(hy.reshape(b, n_ctx, d), wt_qkv, gq, gk, cos_t, sin_t, rope=False,
                        q_scale=1.0)
    lambda_init = 0.8 - 0.6 * math.exp(-0.3 * 0)
    o = _attn_call(qt, k, vt, ky, vyt, attn_lam_q1[0:1], attn_lam_k1[0:1], attn_lam_q2[0:1],
                   attn_lam_k2[0:1], attn_sub_g[0:1], gq, gk, lambda_init=lambda_init,
                   q_scale=q_scale)
    x = _ffn_call(x, mx[0], norm_g[0], wgu2, wd2, 0, ffn_idx=1, mix=(o, w_o))

    x, hx = _ffn_call(x, mx[1], norm_g[1], wgu1, wd1, 1, ffn_idx=0, emit_h=True)
    f = _fourier_call(hx, *_dft_tables(s))
    x = _ffn_call(x, mx[1], norm_g[1], wgu2, wd2, 1, ffn_idx=1, mix=(f, w_f))
    return x
```
